```python
import jax, jax.numpy as jnp
from jax import lax
import numpy as np

D_MODEL = 1024
BATCH = 32
SEQ = 2048
DEPTH = 2
DEC_BATCH = 16
DEC_SEQ = 4096
PAST_LEN = 128

N_MIXERS = 2
N_Q_HEADS = 8
N_KV_HEADS = 2
HEAD_DIM = 128
GQA_GROUP = N_Q_HEADS // N_KV_HEADS
QKV_DIM = (N_Q_HEADS + 2 * N_KV_HEADS) * HEAD_DIM
ROPE_HALF = HEAD_DIM // 2
ROPE_THETA = 10000.0
Q_BLOCK = 128
GRID_W = 64
GMLP_WIDTH = D_MODEL
GMLP_GROUPS = 8
CHUNK = 128
D_FF = 2816
PLE_DIM = 256
EPS = 1e-6
N_ATTN = (DEPTH + 1) // 2
N_GMLP = DEPTH // 2

kernel_name = "hybrid_gqa_gmlp_macaron_encoder"


def rms_norm(x, g):
    xf = x.astype(jnp.float32)
    y = xf * lax.rsqrt(jnp.mean(xf * xf, axis=-1, keepdims=True) + EPS)
    return (y * g.astype(jnp.float32)).astype(x.dtype)


def swiglu(x, w_gate, w_up, w_down):
    return (jax.nn.silu(x @ w_gate) * (x @ w_up)) @ w_down


def axial_angles(seq_len):
    rows = seq_len // GRID_W
    row = jnp.repeat(jnp.arange(rows, dtype=jnp.float32), GRID_W)
    col = jnp.tile(jnp.arange(GRID_W, dtype=jnp.float32), rows)
    inv = ROPE_THETA ** (-jnp.arange(0, ROPE_HALF, 2, dtype=jnp.float32) / ROPE_HALF)
    return row[:, None] * inv, col[:, None] * inv


def rope_1d(x, ang):
    c = jnp.cos(ang)[None, :, None, :]
    s = jnp.sin(ang)[None, :, None, :]
    x1, x2 = jnp.split(x, 2, axis=-1)
    return jnp.concatenate([x1 * c - x2 * s, x1 * s + x2 * c], axis=-1)


def axial_rope(x, ang_row, ang_col):
    xf = x.astype(jnp.float32)
    out = jnp.concatenate([rope_1d(xf[..., :ROPE_HALF], ang_row),
                           rope_1d(xf[..., ROPE_HALF:], ang_col)], axis=-1)
    return out.astype(x.dtype)


def attention_mixer(h, w_qkv, q_norm, k_norm, w_o):
    b, s, _ = h.shape
    qkv = h @ w_qkv
    q, k, v = jnp.split(qkv, [N_Q_HEADS * HEAD_DIM, (N_Q_HEADS + N_KV_HEADS) * HEAD_DIM], axis=-1)
    q = rms_norm(q.reshape(b, s, N_Q_HEADS, HEAD_DIM), q_norm)
    k = rms_norm(k.reshape(b, s, N_KV_HEADS, HEAD_DIM), k_norm)
    v = v.reshape(b, s, N_KV_HEADS, HEAD_DIM)
    ang_r, ang_c = axial_angles(s)
    q = axial_rope(q, ang_r, ang_c) * (HEAD_DIM ** -0.5)
    k = axial_rope(k, ang_r, ang_c)
    nb = s // Q_BLOCK
    qb = q.reshape(b, nb, Q_BLOCK, N_KV_HEADS, GQA_GROUP, HEAD_DIM).transpose(1, 0, 2, 3, 4, 5)

    def block(qi):
        sc = jnp.einsum('bqhgd,bkhd->bhgqk', qi, k, preferred_element_type=jnp.float32)
        pr = jax.nn.softmax(sc, axis=-1).astype(v.dtype)
        return jnp.einsum('bhgqk,bkhd->bqhgd', pr, v)

    o = lax.map(block, qb)
    o = o.transpose(1, 0, 2, 3, 4, 5).reshape(b, s, N_Q_HEADS * HEAD_DIM)
    return o @ w_o


def gmlp_mixer(h, w_in, v_norm, w_s, b_s, w_out):
    b, s, _ = h.shape
    z = jax.nn.gelu(h @ w_in, approximate=False)
    u, v = jnp.split(z, 2, axis=-1)
    v = rms_norm(v, v_norm)
    vc = v.reshape(b, s // CHUNK, CHUNK, GMLP_GROUPS, GMLP_WIDTH // GMLP_GROUPS)
    sv = jnp.einsum('gpq,bnqgc->bnpgc', w_s, vc) + b_s.T[None, None, :, :, None]
    return (u * sv.reshape(b, s, GMLP_WIDTH)) @ w_out


def trunk(x, p, ffn1_norm, ffn1_w_gate, ffn1_w_up, ffn1_w_down, mix_norm,
          attn_w_qkv, attn_q_norm, attn_k_norm, attn_w_o,
          gmlp_w_in, gmlp_v_norm, gmlp_w_s, gmlp_b_s, gmlp_w_out,
          ffn2_norm, ffn2_w_gate, ffn2_w_up, ffn2_w_down,
          ple_norm, ple_w_gate, ple_w_proj):
    h = x
    for i in range(DEPTH):
        h = h + 0.5 * swiglu(rms_norm(h, ffn1_norm[i]), ffn1_w_gate[i], ffn1_w_up[i], ffn1_w_down[i])
        hn = rms_norm(h, mix_norm[i])
        j = i // N_MIXERS
        if i % N_MIXERS == 0:
            h = h + attention_mixer(hn, attn_w_qkv[j], attn_q_norm[j], attn_k_norm[j], attn_w_o[j])
        else:
            h = h + gmlp_mixer(hn, gmlp_w_in[j], gmlp_v_norm[j], gmlp_w_s[j], gmlp_b_s[j], gmlp_w_out[j])
        h = h + 0.5 * swiglu(rms_norm(h, ffn2_norm[i]), ffn2_w_gate[i], ffn2_w_up[i], ffn2_w_down[i])
        gate = jax.nn.sigmoid(rms_norm(h, ple_norm[i]) @ ple_w_gate[i])
        h = h + gate * (p[i] @ ple_w_proj[i])
    return h


def setup_inputs(seed: int = 0) -> dict:
    key = jax.random.key(seed)
    ks = jax.random.split(key, 32)
    f32 = jnp.float32

    def nrm(k, shape, scale):
        return jax.random.normal(k, shape, dtype=f32) * scale

    def gain(k, shape):
        return 1.0 + 0.01 * jax.random.normal(k, shape, dtype=f32)

    return {
        "x_prompt": nrm(ks[0], (BATCH, SEQ, D_MODEL), 1.0),
        "x_sample": nrm(ks[1], (DEC_BATCH, DEC_SEQ, D_MODEL), 1.0),
        "p_prompt": nrm(ks[2], (DEPTH, BATCH, SEQ, PLE_DIM), 1.0),
        "p_sample": nrm(ks[3], (DEPTH, DEC_BATCH, DEC_SEQ, PLE_DIM), 1.0),
        "ffn1_norm": gain(ks[4], (DEPTH, D_MODEL)),
        "ffn1_w_gate": nrm(ks[5], (DEPTH, D_MODEL, D_FF), D_MODEL ** -0.5),
        "ffn1_w_up": nrm(ks[6], (DEPTH, D_MODEL, D_FF), D_MODEL ** -0.5),
        "ffn1_w_down": nrm(ks[7], (DEPTH, D_FF, D_MODEL), D_FF ** -0.5),
        "mix_norm": gain(ks[8], (DEPTH, D_MODEL)),
        "attn_w_qkv": nrm(ks[9], (N_ATTN, D_MODEL, QKV_DIM), D_MODEL ** -0.5),
        "attn_q_norm": gain(ks[10], (N_ATTN, HEAD_DIM)),
        "attn_k_norm": gain(ks[11], (N_ATTN, HEAD_DIM)),
        "attn_w_o": nrm(ks[12], (N_ATTN, N_Q_HEADS * HEAD_DIM, D_MODEL), (N_Q_HEADS * HEAD_DIM) ** -0.5),
        "gmlp_w_in": nrm(ks[13], (N_GMLP, D_MODEL, 2 * GMLP_WIDTH), D_MODEL ** -0.5),
        "gmlp_v_norm": gain(ks[14], (N_GMLP, GMLP_WIDTH)),
        "gmlp_w_s": nrm(ks[15], (N_GMLP, GMLP_GROUPS, CHUNK, CHUNK), CHUNK ** -0.5),
        "gmlp_b_s": 1.0 + 0.1 * jax.random.normal(ks[16], (N_GMLP, GMLP_GROUPS, CHUNK), dtype=f32),
        "gmlp_w_out": nrm(ks[17], (N_GMLP, GMLP_WIDTH, D_MODEL), GMLP_WIDTH ** -0.5),
        "ffn2_norm": gain(ks[18], (DEPTH, D_MODEL)),
        "ffn2_w_gate": nrm(ks[19], (DEPTH, D_MODEL, D_FF), D_MODEL ** -0.5),
        "ffn2_w_up": nrm(ks[20], (DEPTH, D_MODEL, D_FF), D_MODEL ** -0.5),
        "ffn2_w_down": nrm(ks[21], (DEPTH, D_FF, D_MODEL), D_FF ** -0.5),
        "ple_norm": gain(ks[22], (DEPTH, D_MODEL)),
        "ple_w_gate": nrm(ks[23], (DEPTH, D_MODEL, D_MODEL), D_MODEL ** -0.5),
        "ple_w_proj": nrm(ks[24], (DEPTH, PLE_DIM, D_MODEL), PLE_DIM ** -0.5),
    }


def reference(x_prompt, x_sample, p_prompt, p_sample,
              ffn1_norm, ffn1_w_gate, ffn1_w_up, ffn1_w_down, mix_norm,
              attn_w_qkv, attn_q_norm, attn_k_norm, attn_w_o,
              gmlp_w_in, gmlp_v_norm, gmlp_w_s, gmlp_b_s, gmlp_w_out,
              ffn2_norm, ffn2_w_gate, ffn2_w_up, ffn2_w_down,
              ple_norm, ple_w_gate, ple_w_proj):
    y_prompt = trunk(x_prompt, p_prompt, ffn1_norm, ffn1_w_gate, ffn1_w_up, ffn1_w_down, mix_norm,
                     attn_w_qkv, attn_q_norm, attn_k_norm, attn_w_o,
                     gmlp_w_in, gmlp_v_norm, gmlp_w_s, gmlp_b_s, gmlp_w_out,
                     ffn2_norm, ffn2_w_gate, ffn2_w_up, ffn2_w_down,
                     ple_norm, ple_w_gate, ple_w_proj)
    y_sample = trunk(x_sample, p_sample, ffn1_norm, ffn1_w_gate, ffn1_w_up, ffn1_w_down, mix_norm,
                     attn_w_qkv, attn_q_norm, attn_k_norm, attn_w_o,
                     gmlp_w_in, gmlp_v_norm, gmlp_w_s, gmlp_b_s, gmlp_w_out,
                     ffn2_norm, ffn2_w_gate, ffn2_w_up, ffn2_w_down,
                     ple_norm, ple_w_gate, ple_w_proj)
    return (y_prompt, y_sample)
```

```python
import functools

import jax
import jax.numpy as jnp
from jax import lax
from jax.experimental import pallas as pl
from jax.experimental.pallas import tpu as pltpu

D_MODEL = 1024
N_Q_HEADS = 8
N_KV_HEADS = 2
HEAD_DIM = 128
GQA_GROUP = N_Q_HEADS // N_KV_HEADS
Q_DIM = N_Q_HEADS * HEAD_DIM
KV_DIM = N_KV_HEADS * HEAD_DIM
QKV_DIM = Q_DIM + 2 * KV_DIM
ROPE_HALF = HEAD_DIM // 2
ROPE_QUARTER = ROPE_HALF // 2
ROPE_THETA = 10000.0
GRID_W = 64
GMLP_WIDTH = D_MODEL
GMLP_GROUPS = 8
GMLP_GROUP_W = GMLP_WIDTH // GMLP_GROUPS
CHUNK = 128
D_FF = 2816
PLE_DIM = 256
EPS = 1e-6

BF16 = jnp.bfloat16
F32 = jnp.float32

VMEM_LIMIT_BYTES = 56 * 1024 * 1024
TOKEN_TILE = 512
ATTN_Q_TILE = 128


def _dot(a, b):
    return jnp.dot(a, b, preferred_element_type=F32)


def _rms(x, g):
    return x * lax.rsqrt(jnp.mean(x * x, axis=-1, keepdims=True) + EPS) * g


def _gelu_exact(x):
    return 0.5 * x * (1.0 + lax.erf(x * (2.0 ** -0.5)))


def _ffn_stage(h, norm_ref, wg_ref, wu_ref, wd_ref):
    xn = _rms(h, norm_ref[...]).astype(BF16)
    g = _dot(xn, wg_ref[...])
    u = _dot(xn, wu_ref[...])
    a = (g * jax.nn.sigmoid(g) * u).astype(BF16)
    return h + 0.5 * _dot(a, wd_ref[...])


def _ple_stage(h, norm_ref, wgate_ref, p, wproj_ref):
    xn = _rms(h, norm_ref[...]).astype(BF16)
    gate = jax.nn.sigmoid(_dot(xn, wgate_ref[...]))
    return h + gate * _dot(p.astype(BF16), wproj_ref[...])


def _rope_partner(x):
    lane = lax.broadcasted_iota(jnp.int32, x.shape, 1)
    first = (lane % ROPE_HALF) < ROPE_QUARTER
    up = pltpu.roll(x, HEAD_DIM - ROPE_QUARTER, 1)
    down = pltpu.roll(x, ROPE_QUARTER, 1)
    return jnp.where(first, up, down)


def _qk_head(x, gain, cos, sin, scale):
    y = _rms(x, gain)
    y = y * cos + _rope_partner(y) * sin
    if scale is not None:
        y = y * scale
    return y.astype(BF16)


def _qkv_stage(h, mixnorm_ref, wqkv_ref, qnorm_ref, knorm_ref, cos_ref, sin_ref,
               q_ref, k_ref, v_ref):
    xn = _rms(h, mixnorm_ref[...]).astype(BF16)
    qkv = _dot(xn, wqkv_ref[...])
    cos = cos_ref[...]
    sin = sin_ref[...]
    for hd in range(N_Q_HEADS):
        sl = slice(hd * HEAD_DIM, (hd + 1) * HEAD_DIM)
        q_ref[:, sl] = _qk_head(qkv[:, sl], qnorm_ref[...], cos, sin, HEAD_DIM ** -0.5)
    for hd in range(N_KV_HEADS):
        sl = slice(hd * HEAD_DIM, (hd + 1) * HEAD_DIM)
        src = slice(Q_DIM + hd * HEAD_DIM, Q_DIM + (hd + 1) * HEAD_DIM)
        k_ref[:, sl] = _qk_head(qkv[:, src], knorm_ref[...], cos, sin, None)
    v_ref[...] = qkv[:, Q_DIM + KV_DIM:].astype(BF16)


def _gmlp_stage(h, mixnorm_ref, win_ref, vnorm_ref, ws_ref, bs_ref, wout_ref):
    tm = h.shape[0]
    n_chunks = tm // CHUNK
    xn = _rms(h, mixnorm_ref[...]).astype(BF16)
    z = _gelu_exact(_dot(xn, win_ref[...]))
    u = z[:, :GMLP_WIDTH]
    v = _rms(z[:, GMLP_WIDTH:], vnorm_ref[...]).astype(BF16)
    cols = []
    for g in range(GMLP_GROUPS):
        gs = slice(g * GMLP_GROUP_W, (g + 1) * GMLP_GROUP_W)
        rhs = jnp.concatenate([v[c * CHUNK:(c + 1) * CHUNK, gs] for c in range(n_chunks)], axis=1)
        cols.append(_dot(ws_ref[g], rhs))
    rows = []
    for c in range(n_chunks):
        cs = slice(c * GMLP_GROUP_W, (c + 1) * GMLP_GROUP_W)
        rows.append(jnp.concatenate([cols[g][:, cs] for g in range(GMLP_GROUPS)], axis=1) + bs_ref[...])
    sv = jnp.concatenate(rows, axis=0)
    return h + _dot((u * sv).astype(BF16), wout_ref[...])


def _pre_attn_kernel(h_ref, n1_ref, wg_ref, wu_ref, wd_ref, mixnorm_ref, wqkv_ref, qnorm_ref, knorm_ref,
                     cos_ref, sin_ref, hout_ref, q_ref, k_ref, v_ref):
    h = _ffn_stage(h_ref[...], n1_ref, wg_ref, wu_ref, wd_ref)
    hout_ref[...] = h
    _qkv_stage(h, mixnorm_ref, wqkv_ref, qnorm_ref, knorm_ref, cos_ref, sin_ref, q_ref, k_ref, v_ref)


def _attn_kernel(q_ref, k_ref, v_ref, o_ref):
    tq = q_ref.shape[1]
    q = jnp.concatenate([q_ref[0, :, g * HEAD_DIM:(g + 1) * HEAD_DIM] for g in range(GQA_GROUP)], axis=0)
    s = lax.dot_general(q, k_ref[0], (((1,), (1,)), ((), ())), preferred_element_type=F32)
    m = jnp.max(s, axis=-1, keepdims=True)
    p = jnp.exp(s - m)
    l = jnp.sum(p, axis=-1, keepdims=True)
    o = _dot(p.astype(BF16), v_ref[0]) / l
    for g in range(GQA_GROUP):
        o_ref[0, :, g * HEAD_DIM:(g + 1) * HEAD_DIM] = o[g * tq:(g + 1) * tq].astype(o_ref.dtype)


def _post_attn_kernel(h_ref, o_ref, wo_ref, n2_ref, wg_ref, wu_ref, wd_ref, pn_ref, pgate_ref, p_ref, pproj_ref,
                      hout_ref):
    h = h_ref[...] + _dot(o_ref[...], wo_ref[...])
    h = _ffn_stage(h, n2_ref, wg_ref, wu_ref, wd_ref)
    hout_ref[...] = _ple_stage(h, pn_ref, pgate_ref, p_ref[...], pproj_ref)


def _ffn_gmlp_kernel(h_ref, n1_ref, wg_ref, wu_ref, wd_ref, mixnorm_ref, win_ref, vnorm_ref, ws_ref, bs_ref,
                     wout_ref, hout_ref):
    h = _ffn_stage(h_ref[...], n1_ref, wg_ref, wu_ref, wd_ref)
    hout_ref[...] = _gmlp_stage(h, mixnorm_ref, win_ref, vnorm_ref, ws_ref, bs_ref, wout_ref)


def _ffn_ple_kernel(h_ref, n2_ref, wg_ref, wu_ref, wd_ref, pn_ref, pgate_ref, p_ref, pproj_ref, hout_ref):
    h = _ffn_stage(h_ref[...], n2_ref, wg_ref, wu_ref, wd_ref)
    hout_ref[...] = _ple_stage(h, pn_ref, pgate_ref, p_ref[...], pproj_ref)


def _resident(arr):
    nd = arr.ndim
    return pl.BlockSpec(arr.shape, lambda *_: (0,) * nd, pipeline_mode=pl.Buffered(1))


def _rows(tm, width):
    return pl.BlockSpec((tm, width), lambda i: (i, 0))


def _params(n_axes=1):
    return pltpu.CompilerParams(dimension_semantics=("parallel",) * n_axes,
                                vmem_limit_bytes=VMEM_LIMIT_BYTES)


def _token_tile(seq_len):
    tm = min(TOKEN_TILE, seq_len)
    assert seq_len % tm == 0 and tm % CHUNK == 0
    return tm


def _pre_attn(h, seq_len, ffn, mixnorm, wqkv, qnorm, knorm, cos, sin):
    t = h.shape[0]
    tm = _token_tile(seq_len)
    tiles_per_seq = seq_len // tm
    weights = (*ffn, mixnorm, wqkv, qnorm, knorm)
    table = pl.BlockSpec((tm, HEAD_DIM), lambda i: (i % tiles_per_seq, 0))
    return pl.pallas_call(
        _pre_attn_kernel,
        grid=(t // tm,),
        in_specs=[_rows(tm, D_MODEL), *[_resident(w) for w in weights], table, table],
        out_specs=[_rows(tm, D_MODEL), _rows(tm, Q_DIM), _rows(tm, KV_DIM), _rows(tm, KV_DIM)],
        out_shape=[jax.ShapeDtypeStruct((t, D_MODEL), F32), jax.ShapeDtypeStruct((t, Q_DIM), BF16),
                   jax.ShapeDtypeStruct((t, KV_DIM), BF16), jax.ShapeDtypeStruct((t, KV_DIM), BF16)],
        compiler_params=_params(),
        name="pre_attn",
    )(h, *weights, cos, sin)


def _attention(q, k, v, batch, seq_len):
    tq = min(ATTN_Q_TILE, seq_len)
    assert seq_len % tq == 0
    gw = GQA_GROUP * HEAD_DIM
    q3 = q.reshape(batch, seq_len, Q_DIM)
    k3 = k.reshape(batch, seq_len, KV_DIM)
    v3 = v.reshape(batch, seq_len, KV_DIM)
    qspec = pl.BlockSpec((1, tq, gw), lambda b, hk, i: (b, i, hk))
    kvspec = pl.BlockSpec((1, seq_len, HEAD_DIM), lambda b, hk, i: (b, 0, hk))
    o = pl.pallas_call(
        _attn_kernel,
        grid=(batch, N_KV_HEADS, seq_len // tq),
        in_specs=[qspec, kvspec, kvspec],
        out_specs=qspec,
        out_shape=jax.ShapeDtypeStruct((batch, seq_len, Q_DIM), BF16),
        compiler_params=_params(3),
        name="attention",
    )(q3, k3, v3)
    return o.reshape(batch * seq_len, Q_DIM)


def _post_attn(h, o, wo, ffn, ple, p, seq_len):
    t = h.shape[0]
    tm = _token_tile(seq_len)
    pn, pgate, pproj = ple
    return pl.pallas_call(
        _post_attn_kernel,
        grid=(t // tm,),
        in_specs=[_rows(tm, D_MODEL), _rows(tm, Q_DIM), _resident(wo), *[_resident(w) for w in ffn],
                  _resident(pn), _resident(pgate), _rows(tm, PLE_DIM), _resident(pproj)],
        out_specs=_rows(tm, D_MODEL),
        out_shape=jax.ShapeDtypeStruct((t, D_MODEL), F32),
        compiler_params=_params(),
        name="post_attn",
    )(h, o, wo, *ffn, pn, pgate, p, pproj)


def _ffn_gmlp(h, ffn, gmlp, seq_len):
    t = h.shape[0]
    tm = _token_tile(seq_len)
    weights = (*ffn, *gmlp)
    return pl.pallas_call(
        _ffn_gmlp_kernel,
        grid=(t // tm,),
        in_specs=[_rows(tm, D_MODEL), *[_resident(w) for w in weights]],
        out_specs=_rows(tm, D_MODEL),
        out_shape=jax.ShapeDtypeStruct((t, D_MODEL), F32),
        compiler_params=_params(),
        name="ffn_gmlp",
    )(h, *weights)


def _ffn_ple(h, ffn, ple, p, seq_len):
    t = h.shape[0]
    tm = _token_tile(seq_len)
    pn, pgate, pproj = ple
    return pl.pallas_call(
        _ffn_ple_kernel,
        grid=(t // tm,),
        in_specs=[_rows(tm, D_MODEL), *[_resident(w) for w in ffn], _resident(pn), _resident(pgate),
                  _rows(tm, PLE_DIM), _resident(pproj)],
        out_specs=_rows(tm, D_MODEL),
        out_shape=jax.ShapeDtypeStruct((t, D_MODEL), F32),
        compiler_params=_params(),
        name="ffn_ple",
    )(h, *ffn, pn, pgate, p, pproj)


def _rope_tables(seq_len):
    t = jnp.arange(seq_len, dtype=jnp.int32)
    row = (t // GRID_W).astype(F32)
    col = (t % GRID_W).astype(F32)
    inv = ROPE_THETA ** (-jnp.arange(0, ROPE_HALF, 2, dtype=F32) / ROPE_HALF)
    ang_r = row[:, None] * inv
    ang_c = col[:, None] * inv
    cos = jnp.concatenate([jnp.cos(ang_r)] * 2 + [jnp.cos(ang_c)] * 2, axis=-1)
    sin = jnp.concatenate([-jnp.sin(ang_r), jnp.sin(ang_r), -jnp.sin(ang_c), jnp.sin(ang_c)], axis=-1)
    return cos, sin


def _row(vec):
    return vec.reshape(1, -1).astype(F32)


def _trunk(x, p, params):
    batch, seq_len, _ = x.shape
    t = batch * seq_len
    h = x.reshape(t, D_MODEL)
    p = p.reshape(p.shape[0], t, PLE_DIM)
    cos, sin = _rope_tables(seq_len)
    l0, l1 = params

    h, q, k, v = _pre_attn(h, seq_len, l0["ffn1"], l0["mix_norm"], l0["w_qkv"], l0["q_norm"], l0["k_norm"], cos, sin)
    o = _attention(q, k, v, batch, seq_len)
    h = _post_attn(h, o, l0["w_o"], l0["ffn2"], l0["ple"], p[0], seq_len)

    h = _ffn_gmlp(h, l1["ffn1"], l1["gmlp"], seq_len)
    h = _ffn_ple(h, l1["ffn2"], l1["ple"], p[1], seq_len)
    return h.reshape(batch, seq_len, D_MODEL)


def kernel(x_prompt, x_sample, p_prompt, p_sample, ffn1_norm, ffn1_w_gate, ffn1_w_up, ffn1_w_down, mix_norm,
           attn_w_qkv, attn_q_norm, attn_k_norm, attn_w_o, gmlp_w_in, gmlp_v_norm, gmlp_w_s, gmlp_b_s, gmlp_w_out,
           ffn2_norm, ffn2_w_gate, ffn2_w_up, ffn2_w_down, ple_norm, ple_w_gate, ple_w_proj):
    def ffn(norm, wg, wu, wd, i):
        return (_row(norm[i]), wg[i].astype(BF16), wu[i].astype(BF16), wd[i].astype(BF16))

    def ple(i):
        return (_row(ple_norm[i]), ple_w_gate[i].astype(BF16), ple_w_proj[i].astype(BF16))

    bias = jnp.repeat(gmlp_b_s[0].T.astype(F32), GMLP_GROUP_W, axis=1)
    layer0 = dict(
        ffn1=ffn(ffn1_norm, ffn1_w_gate, ffn1_w_up, ffn1_w_down, 0),
        mix_norm=_row(mix_norm[0]), w_qkv=attn_w_qkv[0].astype(BF16),
        q_norm=_row(attn_q_norm[0]), k_norm=_row(attn_k_norm[0]), w_o=attn_w_o[0].astype(BF16),
        ffn2=ffn(ffn2_norm, ffn2_w_gate, ffn2_w_up, ffn2_w_down, 0), ple=ple(0))
    layer1 = dict(
        ffn1=ffn(ffn1_norm, ffn1_w_gate, ffn1_w_up, ffn1_w_down, 1),
        gmlp=(_row(mix_norm[1]), gmlp_w_in[0].astype(BF16), _row(gmlp_v_norm[0]), gmlp_w_s[0].astype(BF16), bias,
              gmlp_w_out[0].astype(BF16)),
        ffn2=ffn(ffn2_norm, ffn2_w_gate, ffn2_w_up, ffn2_w_down, 1), ple=ple(1))
    params = (layer0, layer1)
    return (_trunk(x_prompt, p_prompt, params), _trunk(x_sample, p_sample, params))
```

```python
import jax
import jax.numpy as jnp
from jax import lax
from jax.experimental import pallas as pl
from jax.experimental.pallas import tpu as pltpu

D_MODEL = 1024
N_Q_HEADS = 8
N_KV_HEADS = 2
HEAD_DIM = 128
GQA_GROUP = N_Q_HEADS // N_KV_HEADS
Q_DIM = N_Q_HEADS * HEAD_DIM
KV_DIM = N_KV_HEADS * HEAD_DIM
QKV_DIM = Q_DIM + 2 * KV_DIM
ROPE_HALF = HEAD_DIM // 2
ROPE_QUARTER = ROPE_HALF // 2
ROPE_THETA = 10000.0
GRID_W = 64
GMLP_WIDTH = D_MODEL
GMLP_GROUPS = 8
GMLP_GROUP_W = GMLP_WIDTH // GMLP_GROUPS
CHUNK = 128
D_FF = 2816
PLE_DIM = 256
EPS = 1e-6

BF16 = jnp.bfloat16
F32 = jnp.float32

VMEM_LIMIT_BYTES = 56 * 1024 * 1024
TOKEN_TILE = 512
ATTN_Q_TILE = 512
ATTN_KV_CHUNK = 256
PRE_ATTN_ROW_BLOCKS = 2
ROW_BLOCKS = 1


def _dot(a, b):
    return jnp.dot(a, b, preferred_element_type=F32)


def _rms(x, g):
    return x * lax.rsqrt(jnp.mean(x * x, axis=-1, keepdims=True) + EPS) * g


def _gelu_exact(x):
    return 0.5 * x * (1.0 + lax.erf(x * (2.0 ** -0.5)))


def _ffn_stage(h, norm_ref, wg_ref, wu_ref, wd_ref):
    xn = _rms(h, norm_ref[...]).astype(BF16)
    g = _dot(xn, wg_ref[...])
    u = _dot(xn, wu_ref[...])
    a = (g * jax.nn.sigmoid(g) * u).astype(BF16)
    return h + 0.5 * _dot(a, wd_ref[...])


def _ple_stage(h, norm_ref, wgate_ref, p, wproj_ref):
    xn = _rms(h, norm_ref[...]).astype(BF16)
    gate = jax.nn.sigmoid(_dot(xn, wgate_ref[...]))
    return h + gate * _dot(p.astype(BF16), wproj_ref[...])


def _rope_partner(x):
    lane = lax.broadcasted_iota(jnp.int32, x.shape, 1)
    first = (lane % ROPE_HALF) < ROPE_QUARTER
    up = pltpu.roll(x, HEAD_DIM - ROPE_QUARTER, 1)
    down = pltpu.roll(x, ROPE_QUARTER, 1)
    return jnp.where(first, up, down)


def _qk_head(x, gain, cos, sin, scale):
    y = _rms(x, gain)
    y = y * cos + _rope_partner(y) * sin
    if scale is not None:
        y = y * scale
    return y.astype(BF16)


def _qkv_stage(h, rows, mixnorm_ref, wqkv_ref, qnorm_ref, knorm_ref, cos_ref, sin_ref,
               q_ref, k_ref, v_ref):
    xn = _rms(h, mixnorm_ref[...]).astype(BF16)
    qkv = _dot(xn, wqkv_ref[...])
    cos = cos_ref[rows, :]
    sin = sin_ref[rows, :]
    for hd in range(N_Q_HEADS):
        sl = slice(hd * HEAD_DIM, (hd + 1) * HEAD_DIM)
        q_ref[rows, sl] = _qk_head(qkv[:, sl], qnorm_ref[...], cos, sin, HEAD_DIM ** -0.5)
    for hd in range(N_KV_HEADS):
        sl = slice(hd * HEAD_DIM, (hd + 1) * HEAD_DIM)
        src = slice(Q_DIM + hd * HEAD_DIM, Q_DIM + (hd + 1) * HEAD_DIM)
        k_ref[rows, sl] = _qk_head(qkv[:, src], knorm_ref[...], cos, sin, None)
    ones = jnp.ones((h.shape[0], HEAD_DIM), BF16)
    for hd in range(N_KV_HEADS):
        src = slice(Q_DIM + KV_DIM + hd * HEAD_DIM, Q_DIM + KV_DIM + (hd + 1) * HEAD_DIM)
        v_ref[rows, 2 * hd * HEAD_DIM:(2 * hd + 1) * HEAD_DIM] = qkv[:, src].astype(BF16)
        v_ref[rows, (2 * hd + 1) * HEAD_DIM:(2 * hd + 2) * HEAD_DIM] = ones


def _gmlp_stage(h, mixnorm_ref, win_ref, vnorm_ref, ws_ref, bs_ref, wout_ref):
    tm = h.shape[0]
    n_chunks = tm // CHUNK
    xn = _rms(h, mixnorm_ref[...]).astype(BF16)
    z = _gelu_exact(_dot(xn, win_ref[...]))
    u = z[:, :GMLP_WIDTH]
    v = _rms(z[:, GMLP_WIDTH:], vnorm_ref[...]).astype(BF16)
    cols = []
    for g in range(GMLP_GROUPS):
        gs = slice(g * GMLP_GROUP_W, (g + 1) * GMLP_GROUP_W)
        rhs = jnp.concatenate([v[c * CHUNK:(c + 1) * CHUNK, gs] for c in range(n_chunks)], axis=1)
        cols.append(_dot(ws_ref[g], rhs))
    rows = []
    for c in range(n_chunks):
        cs = slice(c * GMLP_GROUP_W, (c + 1) * GMLP_GROUP_W)
        rows.append(jnp.concatenate([cols[g][:, cs] for g in range(GMLP_GROUPS)], axis=1) + bs_ref[...])
    sv = jnp.concatenate(rows, axis=0)
    return h + _dot((u * sv).astype(BF16), wout_ref[...])


def _row_blocks(ref, n_blocks=ROW_BLOCKS):
    n = ref.shape[0] // n_blocks
    return [slice(i * n, (i + 1) * n) for i in range(n_blocks)]


def _pre_attn_kernel(h_ref, n1_ref, wg_ref, wu_ref, wd_ref, mixnorm_ref, wqkv_ref, qnorm_ref, knorm_ref,
                     cos_ref, sin_ref, hout_ref, q_ref, k_ref, v_ref):
    for rows in _row_blocks(h_ref, PRE_ATTN_ROW_BLOCKS):
        h = _ffn_stage(h_ref[rows, :], n1_ref, wg_ref, wu_ref, wd_ref)
        hout_ref[rows, :] = h
        _qkv_stage(h, rows, mixnorm_ref, wqkv_ref, qnorm_ref, knorm_ref, cos_ref, sin_ref, q_ref, k_ref, v_ref)


def _attn_kernel(q_ref, k_ref, v_ref, o_ref):
    tq = q_ref.shape[1]
    seq_len = k_ref.shape[1]
    tk = min(ATTN_KV_CHUNK, seq_len)
    q = jnp.concatenate([q_ref[0, :, g * HEAD_DIM:(g + 1) * HEAD_DIM] for g in range(GQA_GROUP)], axis=0)
    m = acc = None
    for j in range(seq_len // tk):
        ks = slice(j * tk, (j + 1) * tk)
        s = lax.dot_general(q, k_ref[0, ks, :], (((1,), (1,)), ((), ())), preferred_element_type=F32)
        m_chunk = jnp.max(s, axis=-1, keepdims=True)
        m_new = m_chunk if m is None else jnp.maximum(m, m_chunk)
        pv = _dot(jnp.exp(s - m_new).astype(BF16), v_ref[0, ks, :])
        acc = pv if m is None else jnp.exp(m - m_new) * acc + pv
        m = m_new
    o = acc[:, :HEAD_DIM] / acc[:, HEAD_DIM:]
    for g in range(GQA_GROUP):
        o_ref[0, :, g * HEAD_DIM:(g + 1) * HEAD_DIM] = o[g * tq:(g + 1) * tq].astype(o_ref.dtype)


def _post_attn_kernel(h_ref, o_ref, wo_ref, n2_ref, wg_ref, wu_ref, wd_ref, pn_ref, pgate_ref, p_ref, pproj_ref,
                      hout_ref):
    for rows in _row_blocks(h_ref):
        h = h_ref[rows, :] + _dot(o_ref[rows, :], wo_ref[...])
        h = _ffn_stage(h, n2_ref, wg_ref, wu_ref, wd_ref)
        hout_ref[rows, :] = _ple_stage(h, pn_ref, pgate_ref, p_ref[rows, :], pproj_ref)


def _ffn_gmlp_kernel(h_ref, n1_ref, wg_ref, wu_ref, wd_ref, mixnorm_ref, win_ref, vnorm_ref, ws_ref, bs_ref,
                     wout_ref, hout_ref):
    for rows in _row_blocks(h_ref):
        h = _ffn_stage(h_ref[rows, :], n1_ref, wg_ref, wu_ref, wd_ref)
        hout_ref[rows, :] = _gmlp_stage(h, mixnorm_ref, win_ref, vnorm_ref, ws_ref, bs_ref, wout_ref)


def _ffn_ple_kernel(h_ref, n2_ref, wg_ref, wu_ref, wd_ref, pn_ref, pgate_ref, p_ref, pproj_ref, hout_ref):
    for rows in _row_blocks(h_ref):
        h = _ffn_stage(h_ref[rows, :], n2_ref, wg_ref, wu_ref, wd_ref)
        hout_ref[rows, :] = _ple_stage(h, pn_ref, pgate_ref, p_ref[rows, :], pproj_ref)


def _resident(arr):
    nd = arr.ndim
    return pl.BlockSpec(arr.shape, lambda *_: (0,) * nd, pipeline_mode=pl.Buffered(1))


def _rows(tm, width):
    return pl.BlockSpec((tm, width), lambda i: (i, 0))


def _params(n_axes=1):
    return pltpu.CompilerParams(dimension_semantics=("parallel",) * n_axes,
                                vmem_limit_bytes=VMEM_LIMIT_BYTES)


def _token_tile(seq_len):
    tm = min(TOKEN_TILE, seq_len)
    assert seq_len % tm == 0 and tm % CHUNK == 0
    return tm


def _pre_attn(h, seq_len, ffn, mixnorm, wqkv, qnorm, knorm, cos, sin):
    t = h.shape[0]
    tm = _token_tile(seq_len)
    tiles_per_seq = seq_len // tm
    weights = (*ffn, mixnorm, wqkv, qnorm, knorm)
    table = pl.BlockSpec((tm, HEAD_DIM), lambda i: (i % tiles_per_seq, 0))
    return pl.pallas_call(
        _pre_attn_kernel,
        grid=(t // tm,),
        in_specs=[_rows(tm, D_MODEL), *[_resident(w) for w in weights], table, table],
        out_specs=[_rows(tm, D_MODEL), _rows(tm, Q_DIM), _rows(tm, KV_DIM), _rows(tm, 2 * KV_DIM)],
        out_shape=[jax.ShapeDtypeStruct((t, D_MODEL), F32), jax.ShapeDtypeStruct((t, Q_DIM), BF16),
                   jax.ShapeDtypeStruct((t, KV_DIM), BF16), jax.ShapeDtypeStruct((t, 2 * KV_DIM), BF16)],
        compiler_params=_params(),
        name="pre_attn",
    )(h, *weights, cos, sin)


def _attention(q, k, v, batch, seq_len):
    tq = min(ATTN_Q_TILE, seq_len)
    assert seq_len % tq == 0
    gw = GQA_GROUP * HEAD_DIM
    q3 = q.reshape(batch, seq_len, Q_DIM)
    k3 = k.reshape(batch, seq_len, KV_DIM)
    v3 = v.reshape(batch, seq_len, 2 * KV_DIM)
    qspec = pl.BlockSpec((1, tq, gw), lambda b, hk, i: (b, i, hk))
    kspec = pl.BlockSpec((1, seq_len, HEAD_DIM), lambda b, hk, i: (b, 0, hk))
    vspec = pl.BlockSpec((1, seq_len, 2 * HEAD_DIM), lambda b, hk, i: (b, 0, hk))
    o = pl.pallas_call(
        _attn_kernel,
        grid=(batch, N_KV_HEADS, seq_len // tq),
        in_specs=[qspec, kspec, vspec],
        out_specs=qspec,
        out_shape=jax.ShapeDtypeStruct((batch, seq_len, Q_DIM), BF16),
        compiler_params=_params(3),
        name="attention",
    )(q3, k3, v3)
    return o.reshape(batch * seq_len, Q_DIM)


def _post_attn(h, o, wo, ffn, ple, p, seq_len):
    t = h.shape[0]
    tm = _token_tile(seq_len)
    pn, pgate, pproj = ple
    return pl.pallas_call(
        _post_attn_kernel,
        grid=(t // tm,),
        in_specs=[_rows(tm, D_MODEL), _rows(tm, Q_DIM), _resident(wo), *[_resident(w) for w in ffn],
                  _resident(pn), _resident(pgate), _rows(tm, PLE_DIM), _resident(pproj)],
        out_specs=_rows(tm, D_MODEL),
        out_shape=jax.ShapeDtypeStruct((t, D_MODEL), F32),
        compiler_params=_params(),
        name="post_attn",
    )(h, o, wo, *ffn, pn, pgate, p, pproj)


def _ffn_gmlp(h, ffn, gmlp, seq_len):
    t = h.shape[0]
    tm = _token_tile(seq_len)
    weights = (*ffn, *gmlp)
    return pl.pallas_call(
        _ffn_gmlp_kernel,
        grid=(t // tm,),
        in_specs=[_rows(tm, D_MODEL), *[_resident(w) for w in weights]],
        out_specs=_rows(tm, D_MODEL),
        out_shape=jax.ShapeDtypeStruct((t, D_MODEL), F32),
        compiler_params=_params(),
        name="ffn_gmlp",
    )(h, *weights)


def _ffn_ple(h, ffn, ple, p, seq_len):
    t = h.shape[0]
    tm = _token_tile(seq_len)
    pn, pgate, pproj = ple
    return pl.pallas_call(
        _ffn_ple_kernel,
        grid=(t // tm,),
        in_specs=[_rows(tm, D_MODEL), *[_resident(w) for w in ffn], _resident(pn), _resident(pgate),
                  _rows(tm, PLE_DIM), _resident(pproj)],
        out_specs=_rows(tm, D_MODEL),
        out_shape=jax.ShapeDtypeStruct((t, D_MODEL), F32),
        compiler_params=_params(),
        name="ffn_ple",
    )(h, *ffn, pn, pgate, p, pproj)


def _rope_tables(seq_len):
    t = jnp.arange(seq_len, dtype=jnp.int32)
    row = (t // GRID_W).astype(F32)
    col = (t % GRID_W).astype(F32)
    inv = ROPE_THETA ** (-jnp.arange(0, ROPE_HALF, 2, dtype=F32) / ROPE_HALF)
    ang_r = row[:, None] * inv
    ang_c = col[:, None] * inv
    cos = jnp.concatenate([jnp.cos(ang_r)] * 2 + [jnp.cos(ang_c)] * 2, axis=-1)
    sin = jnp.concatenate([-jnp.sin(ang_r), jnp.sin(ang_r), -jnp.sin(ang_c), jnp.sin(ang_c)], axis=-1)
    return cos, sin


def _row(vec):
    return vec.reshape(1, -1).astype(F32)


def _trunk(x, p, params):
    batch, seq_len, _ = x.shape
    t = batch * seq_len
    h = x.reshape(t, D_MODEL)
    p = p.reshape(p.shape[0], t, PLE_DIM)
    cos, sin = _rope_tables(seq_len)
    l0, l1 = params

    h, q, k, v = _pre_attn(h, seq_len, l0["ffn1"], l0["mix_norm"], l0["w_qkv"], l0["q_norm"], l0["k_norm"], cos, sin)
    o = _attention(q, k, v, batch, seq_len)
    h = _post_attn(h, o, l0["w_o"], l0["ffn2"], l0["ple"], p[0], seq_len)

    h = _ffn_gmlp(h, l1["ffn1"], l1["gmlp"], seq_len)
    h = _ffn_ple(h, l1["ffn2"], l1["ple"], p[1], seq_len)
    return h.reshape(batch, seq_len, D_MODEL)


def kernel(x_prompt, x_sample, p_prompt, p_sample, ffn1_norm, ffn1_w_gate, ffn1_w_up, ffn1_w_down, mix_norm,
           attn_w_qkv, attn_q_norm, attn_k_norm, attn_w_o, gmlp_w_in, gmlp_v_norm, gmlp_w_s, gmlp_b_s, gmlp_w_out,
           ffn2_norm, ffn2_w_gate, ffn2_w_up, ffn2_w_down, ple_norm, ple_w_gate, ple_w_proj):
    def ffn(norm, wg, wu, wd, i):
        return (_row(norm[i]), wg[i].astype(BF16), wu[i].astype(BF16), wd[i].astype(BF16))

    def ple(i):
        return (_row(ple_norm[i]), ple_w_gate[i].astype(BF16), ple_w_proj[i].astype(BF16))

    bias = jnp.repeat(gmlp_b_s[0].T.astype(F32), GMLP_GROUP_W, axis=1)
    layer0 = dict(
        ffn1=ffn(ffn1_norm, ffn1_w_gate, ffn1_w_up, ffn1_w_down, 0),
        mix_norm=_row(mix_norm[0]), w_qkv=attn_w_qkv[0].astype(BF16),
        q_norm=_row(attn_q_norm[0]), k_norm=_row(attn_k_norm[0]), w_o=attn_w_o[0].astype(BF16),
        ffn2=ffn(ffn2_norm, ffn2_w_gate, ffn2_w_up, ffn2_w_down, 0), ple=ple(0))
    layer1 = dict(
        ffn1=ffn(ffn1_norm, ffn1_w_gate, ffn1_w_up, ffn1_w_down, 1),
        gmlp=(_row(mix_norm[1]), gmlp_w_in[0].astype(BF16), _row(gmlp_v_norm[0]), gmlp_w_s[0].astype(BF16), bias,
              gmlp_w_out[0].astype(BF16)),
        ffn2=ffn(ffn2_norm, ffn2_w_gate, ffn2_w_up, ffn2_w_down, 1), ple=ple(1))
    params = (layer0, layer1)
    return (_trunk(x_prompt, p_prompt, params), _trunk(x_sample, p_sample, params))
```

```python
import jax
import jax.numpy as jnp
from jax import lax
from jax.experimental import pallas as pl
from jax.experimental.pallas import tpu as pltpu

D_MODEL = 1024
N_Q_HEADS = 8
N_KV_HEADS = 2
HEAD_DIM = 128
GQA_GROUP = N_Q_HEADS // N_KV_HEADS
Q_DIM = N_Q_HEADS * HEAD_DIM
KV_DIM = N_KV_HEADS * HEAD_DIM
QKV_DIM = Q_DIM + 2 * KV_DIM
ROPE_HALF = HEAD_DIM // 2
ROPE_QUARTER = ROPE_HALF // 2
ROPE_THETA = 10000.0
GRID_W = 64
GMLP_WIDTH = D_MODEL
GMLP_GROUPS = 8
GMLP_GROUP_W = GMLP_WIDTH // GMLP_GROUPS
CHUNK = 128
D_FF = 2816
PLE_DIM = 256
EPS = 1e-6

BF16 = jnp.bfloat16
F32 = jnp.float32

VMEM_LIMIT_BYTES = 56 * 1024 * 1024
TOKEN_TILE = 512
ATTN_Q_TILE = 512
ATTN_KV_CHUNK = 256
ROW_BLOCKS = 2


def _dot(a, b):
    return jnp.dot(a, b, preferred_element_type=F32)


def _rms(x, g):
    return x * lax.rsqrt(jnp.mean(x * x, axis=-1, keepdims=True) + EPS) * g


def _gelu_exact(x):
    return 0.5 * x * (1.0 + lax.erf(x * (2.0 ** -0.5)))


def _ffn_stage(hs, norm_ref, wg_ref, wu_ref, wd_ref):
    xns = [_rms(h, norm_ref[...]).astype(BF16) for h in hs]
    gs = [_dot(xn, wg_ref[...]) for xn in xns]
    us = [_dot(xn, wu_ref[...]) for xn in xns]
    acts = [(g * jax.nn.sigmoid(g) * u).astype(BF16) for g, u in zip(gs, us)]
    return [h + 0.5 * _dot(a, wd_ref[...]) for h, a in zip(hs, acts)]


def _ple_stage(hs, norm_ref, wgate_ref, ps, wproj_ref):
    xns = [_rms(h, norm_ref[...]).astype(BF16) for h in hs]
    gates = [jax.nn.sigmoid(_dot(xn, wgate_ref[...])) for xn in xns]
    return [h + gate * _dot(p.astype(BF16), wproj_ref[...]) for h, gate, p in zip(hs, gates, ps)]


def _rope_partner(x):
    lane = lax.broadcasted_iota(jnp.int32, x.shape, 1)
    first = (lane % ROPE_HALF) < ROPE_QUARTER
    up = pltpu.roll(x, HEAD_DIM - ROPE_QUARTER, 1)
    down = pltpu.roll(x, ROPE_QUARTER, 1)
    return jnp.where(first, up, down)


def _qk_head(x, gain, cos, sin, scale):
    y = _rms(x, gain)
    y = y * cos + _rope_partner(y) * sin
    if scale is not None:
        y = y * scale
    return y.astype(BF16)


def _qkv_stage(h, rows, mixnorm_ref, wqkv_ref, qnorm_ref, knorm_ref, cos_ref, sin_ref,
               q_ref, k_ref, v_ref):
    xn = _rms(h, mixnorm_ref[...]).astype(BF16)
    qkv = _dot(xn, wqkv_ref[...])
    cos = cos_ref[rows, :]
    sin = sin_ref[rows, :]
    for hd in range(N_Q_HEADS):
        sl = slice(hd * HEAD_DIM, (hd + 1) * HEAD_DIM)
        q_ref[rows, sl] = _qk_head(qkv[:, sl], qnorm_ref[...], cos, sin, HEAD_DIM ** -0.5)
    for hd in range(N_KV_HEADS):
        sl = slice(hd * HEAD_DIM, (hd + 1) * HEAD_DIM)
        src = slice(Q_DIM + hd * HEAD_DIM, Q_DIM + (hd + 1) * HEAD_DIM)
        k_ref[rows, sl] = _qk_head(qkv[:, src], knorm_ref[...], cos, sin, None)
    ones = jnp.ones((h.shape[0], HEAD_DIM), BF16)
    for hd in range(N_KV_HEADS):
        src = slice(Q_DIM + KV_DIM + hd * HEAD_DIM, Q_DIM + KV_DIM + (hd + 1) * HEAD_DIM)
        v_ref[rows, 2 * hd * HEAD_DIM:(2 * hd + 1) * HEAD_DIM] = qkv[:, src].astype(BF16)
        v_ref[rows, (2 * hd + 1) * HEAD_DIM:(2 * hd + 2) * HEAD_DIM] = ones


def _gmlp_stage(h, mixnorm_ref, win_ref, vnorm_ref, ws_ref, bs_ref, wout_ref):
    tm = h.shape[0]
    n_chunks = tm // CHUNK
    xn = _rms(h, mixnorm_ref[...]).astype(BF16)
    z = _gelu_exact(_dot(xn, win_ref[...]))
    u = z[:, :GMLP_WIDTH]
    v = _rms(z[:, GMLP_WIDTH:], vnorm_ref[...]).astype(BF16)
    cols = []
    for g in range(GMLP_GROUPS):
        gs = slice(g * GMLP_GROUP_W, (g + 1) * GMLP_GROUP_W)
        rhs = jnp.concatenate([v[c * CHUNK:(c + 1) * CHUNK, gs] for c in range(n_chunks)], axis=1)
        cols.append(_dot(ws_ref[g], rhs))
    rows = []
    for c in range(n_chunks):
        cs = slice(c * GMLP_GROUP_W, (c + 1) * GMLP_GROUP_W)
        rows.append(jnp.concatenate([cols[g][:, cs] for g in range(GMLP_GROUPS)], axis=1) + bs_ref[...])
    sv = jnp.concatenate(rows, axis=0)
    return h + _dot((u * sv).astype(BF16), wout_ref[...])


def _row_blocks(ref, n_blocks=ROW_BLOCKS):
    n = ref.shape[0] // n_blocks
    return [slice(i * n, (i + 1) * n) for i in range(n_blocks)]


def _pre_attn_kernel(h_ref, n1_ref, wg_ref, wu_ref, wd_ref, mixnorm_ref, wqkv_ref, qnorm_ref, knorm_ref,
                     cos_ref, sin_ref, hout_ref, q_ref, k_ref, v_ref):
    blocks = _row_blocks(h_ref)
    hs = _ffn_stage([h_ref[rows, :] for rows in blocks], n1_ref, wg_ref, wu_ref, wd_ref)
    for rows, h in zip(blocks, hs):
        hout_ref[rows, :] = h
        _qkv_stage(h, rows, mixnorm_ref, wqkv_ref, qnorm_ref, knorm_ref, cos_ref, sin_ref, q_ref, k_ref, v_ref)


def _attn_kernel(q_ref, k_ref, v_ref, o_ref):
    tq = q_ref.shape[1]
    seq_len = k_ref.shape[1]
    tk = min(ATTN_KV_CHUNK, seq_len)
    q = jnp.concatenate([q_ref[0, :, g * HEAD_DIM:(g + 1) * HEAD_DIM] for g in range(GQA_GROUP)], axis=0)
    m = acc = None
    for j in range(seq_len // tk):
        ks = slice(j * tk, (j + 1) * tk)
        s = lax.dot_general(q, k_ref[0, ks, :], (((1,), (1,)), ((), ())), preferred_element_type=F32)
        m_chunk = jnp.max(s, axis=-1, keepdims=True)
        m_new = m_chunk if m is None else jnp.maximum(m, m_chunk)
        pv = _dot(jnp.exp(s - m_new).astype(BF16), v_ref[0, ks, :])
        acc = pv if m is None else jnp.exp(m - m_new) * acc + pv
        m = m_new
    o = acc[:, :HEAD_DIM] / acc[:, HEAD_DIM:]
    for g in range(GQA_GROUP):
        o_ref[0, :, g * HEAD_DIM:(g + 1) * HEAD_DIM] = o[g * tq:(g + 1) * tq].astype(o_ref.dtype)


def _post_attn_kernel(h_ref, o_ref, wo_ref, n2_ref, wg_ref, wu_ref, wd_ref, pn_ref, pgate_ref, p_ref, pproj_ref,
                      hout_ref):
    blocks = _row_blocks(h_ref)
    hs = [h_ref[rows, :] + _dot(o_ref[rows, :], wo_ref[...]) for rows in blocks]
    hs = _ffn_stage(hs, n2_ref, wg_ref, wu_ref, wd_ref)
    hs = _ple_stage(hs, pn_ref, pgate_ref, [p_ref[rows, :] for rows in blocks], pproj_ref)
    for rows, h in zip(blocks, hs):
        hout_ref[rows, :] = h


def _ffn_gmlp_kernel(h_ref, n1_ref, wg_ref, wu_ref, wd_ref, mixnorm_ref, win_ref, vnorm_ref, ws_ref, bs_ref,
                     wout_ref, hout_ref):
    blocks = _row_blocks(h_ref)
    hs = _ffn_stage([h_ref[rows, :] for rows in blocks], n1_ref, wg_ref, wu_ref, wd_ref)
    for rows, h in zip(blocks, hs):
        hout_ref[rows, :] = _gmlp_stage(h, mixnorm_ref, win_ref, vnorm_ref, ws_ref, bs_ref, wout_ref)


def _ffn_ple_kernel(h_ref, n2_ref, wg_ref, wu_ref, wd_ref, pn_ref, pgate_ref, p_ref, pproj_ref, hout_ref):
    blocks = _row_blocks(h_ref)
    hs = _ffn_stage([h_ref[rows, :] for rows in blocks], n2_ref, wg_ref, wu_ref, wd_ref)
    hs = _ple_stage(hs, pn_ref, pgate_ref, [p_ref[rows, :] for rows in blocks], pproj_ref)
    for rows, h in zip(blocks, hs):
        hout_ref[rows, :] = h


def _resident(arr):
    nd = arr.ndim
    return pl.BlockSpec(arr.shape, lambda *_: (0,) * nd, pipeline_mode=pl.Buffered(1))


def _rows(tm, width):
    return pl.BlockSpec((tm, width), lambda i: (i, 0))


def _layer_rows(layer, tm, width):
    return pl.BlockSpec((None, tm, width), lambda i: (layer, i, 0))


def _params(n_axes=1):
    return pltpu.CompilerParams(dimension_semantics=("parallel",) * n_axes,
                                vmem_limit_bytes=VMEM_LIMIT_BYTES)


def _token_tile(seq_len):
    tm = min(TOKEN_TILE, seq_len)
    assert seq_len % tm == 0 and tm % (CHUNK * ROW_BLOCKS) == 0
    return tm


def _pre_attn(h, seq_len, ffn, mixnorm, wqkv, qnorm, knorm, cos, sin):
    t = h.shape[0]
    tm = _token_tile(seq_len)
    tiles_per_seq = seq_len // tm
    weights = (*ffn, mixnorm, wqkv, qnorm, knorm)
    table = pl.BlockSpec((tm, HEAD_DIM), lambda i: (i % tiles_per_seq, 0))
    return pl.pallas_call(
        _pre_attn_kernel,
        grid=(t // tm,),
        in_specs=[_rows(tm, D_MODEL), *[_resident(w) for w in weights], table, table],
        out_specs=[_rows(tm, D_MODEL), _rows(tm, Q_DIM), _rows(tm, KV_DIM), _rows(tm, 2 * KV_DIM)],
        out_shape=[jax.ShapeDtypeStruct((t, D_MODEL), F32), jax.ShapeDtypeStruct((t, Q_DIM), BF16),
                   jax.ShapeDtypeStruct((t, KV_DIM), BF16), jax.ShapeDtypeStruct((t, 2 * KV_DIM), BF16)],
        compiler_params=_params(),
        name="pre_attn",
    )(h, *weights, cos, sin)


def _attention(q, k, v, batch, seq_len):
    tq = min(ATTN_Q_TILE, seq_len)
    assert seq_len % tq == 0
    gw = GQA_GROUP * HEAD_DIM
    q3 = q.reshape(batch, seq_len, Q_DIM)
    k3 = k.reshape(batch, seq_len, KV_DIM)
    v3 = v.reshape(batch, seq_len, 2 * KV_DIM)
    qspec = pl.BlockSpec((1, tq, gw), lambda b, hk, i: (b, i, hk))
    kspec = pl.BlockSpec((1, seq_len, HEAD_DIM), lambda b, hk, i: (b, 0, hk))
    vspec = pl.BlockSpec((1, seq_len, 2 * HEAD_DIM), lambda b, hk, i: (b, 0, hk))
    o = pl.pallas_call(
        _attn_kernel,
        grid=(batch, N_KV_HEADS, seq_len // tq),
        in_specs=[qspec, kspec, vspec],
        out_specs=qspec,
        out_shape=jax.ShapeDtypeStruct((batch, seq_len, Q_DIM), BF16),
        compiler_params=_params(3),
        name="attention",
    )(q3, k3, v3)
    return o.reshape(batch * seq_len, Q_DIM)


def _post_attn(h, o, wo, ffn, ple, p, layer, seq_len):
    t = h.shape[0]
    tm = _token_tile(seq_len)
    pn, pgate, pproj = ple
    return pl.pallas_call(
        _post_attn_kernel,
        grid=(t // tm,),
        in_specs=[_rows(tm, D_MODEL), _rows(tm, Q_DIM), _resident(wo), *[_resident(w) for w in ffn],
                  _resident(pn), _resident(pgate), _layer_rows(layer, tm, PLE_DIM), _resident(pproj)],
        out_specs=_rows(tm, D_MODEL),
        out_shape=jax.ShapeDtypeStruct((t, D_MODEL), F32),
        compiler_params=_params(),
        name="post_attn",
    )(h, o, wo, *ffn, pn, pgate, p, pproj)


def _ffn_gmlp(h, ffn, gmlp, seq_len):
    t = h.shape[0]
    tm = _token_tile(seq_len)
    weights = (*ffn, *gmlp)
    return pl.pallas_call(
        _ffn_gmlp_kernel,
        grid=(t // tm,),
        in_specs=[_rows(tm, D_MODEL), *[_resident(w) for w in weights]],
        out_specs=_rows(tm, D_MODEL),
        out_shape=jax.ShapeDtypeStruct((t, D_MODEL), F32),
        compiler_params=_params(),
        name="ffn_gmlp",
    )(h, *weights)


def _ffn_ple(h, ffn, ple, p, layer, seq_len):
    t = h.shape[0]
    tm = _token_tile(seq_len)
    pn, pgate, pproj = ple
    return pl.pallas_call(
        _ffn_ple_kernel,
        grid=(t // tm,),
        in_specs=[_rows(tm, D_MODEL), *[_resident(w) for w in ffn], _resident(pn), _resident(pgate),
                  _layer_rows(layer, tm, PLE_DIM), _resident(pproj)],
        out_specs=_rows(tm, D_MODEL),
        out_shape=jax.ShapeDtypeStruct((t, D_MODEL), F32),
        compiler_params=_params(),
        name="ffn_ple",
    )(h, *ffn, pn, pgate, p, pproj)


def _rope_tables(seq_len):
    t = jnp.arange(seq_len, dtype=jnp.int32)
    row = (t // GRID_W).astype(F32)
    col = (t % GRID_W).astype(F32)
    inv = ROPE_THETA ** (-jnp.arange(0, ROPE_HALF, 2, dtype=F32) / ROPE_HALF)
    ang_r = row[:, None] * inv
    ang_c = col[:, None] * inv
    cos = jnp.concatenate([jnp.cos(ang_r)] * 2 + [jnp.cos(ang_c)] * 2, axis=-1)
    sin = jnp.concatenate([-jnp.sin(ang_r), jnp.sin(ang_r), -jnp.sin(ang_c), jnp.sin(ang_c)], axis=-1)
    return cos, sin


def _row(vec):
    return vec.reshape(1, -1).astype(F32)


def _trunk(x, p, params):
    batch, seq_len, _ = x.shape
    t = batch * seq_len
    h = x.reshape(t, D_MODEL)
    p = p.reshape(p.shape[0], t, PLE_DIM)
    cos, sin = _rope_tables(seq_len)
    l0, l1 = params

    h, q, k, v = _pre_attn(h, seq_len, l0["ffn1"], l0["mix_norm"], l0["w_qkv"], l0["q_norm"], l0["k_norm"], cos, sin)
    o = _attention(q, k, v, batch, seq_len)
    h = _post_attn(h, o, l0["w_o"], l0["ffn2"], l0["ple"], p, 0, seq_len)

    h = _ffn_gmlp(h, l1["ffn1"], l1["gmlp"], seq_len)
    h = _ffn_ple(h, l1["ffn2"], l1["ple"], p, 1, seq_len)
    return h.reshape(batch, seq_len, D_MODEL)


def kernel(x_prompt, x_sample, p_prompt, p_sample, ffn1_norm, ffn1_w_gate, ffn1_w_up, ffn1_w_down, mix_norm,
           attn_w_qkv, attn_q_norm, attn_k_norm, attn_w_o, gmlp_w_in, gmlp_v_norm, gmlp_w_s, gmlp_b_s, gmlp_w_out,
           ffn2_norm, ffn2_w_gate, ffn2_w_up, ffn2_w_down, ple_norm, ple_w_gate, ple_w_proj):
    def ffn(norm, wg, wu, wd, i):
        return (_row(norm[i]), wg[i].astype(BF16), wu[i].astype(BF16), wd[i].astype(BF16))

    def ple(i):
        return (_row(ple_norm[i]), ple_w_gate[i].astype(BF16), ple_w_proj[i].astype(BF16))

    bias = jnp.repeat(gmlp_b_s[0].T.astype(F32), GMLP_GROUP_W, axis=1)
    layer0 = dict(
        ffn1=ffn(ffn1_norm, ffn1_w_gate, ffn1_w_up, ffn1_w_down, 0),
        mix_norm=_row(mix_norm[0]), w_qkv=attn_w_qkv[0].astype(BF16),
        q_norm=_row(attn_q_norm[0]), k_norm=_row(attn_k_norm[0]), w_o=attn_w_o[0].astype(BF16),
        ffn2=ffn(ffn2_norm, ffn2_w_gate, ffn2_w_up, ffn2_w_down, 0), ple=ple(0))
    layer1 = dict(
        ffn1=ffn(ffn1_norm, ffn1_w_gate, ffn1_w_up, ffn1_w_down, 1),
        gmlp=(_row(mix_norm[1]), gmlp_w_in[0].astype(BF16), _row(gmlp_v_norm[0]), gmlp_w_s[0].astype(BF16), bias,
              gmlp_w_out[0].astype(BF16)),
        ffn2=ffn(ffn2_norm, ffn2_w_gate, ffn2_w_up, ffn2_w_down, 1), ple=ple(1))
    params = (layer0, layer1)
    return (_trunk(x_prompt, p_prompt, params), _trunk(x_sample, p_sample, params))
```

```python
import jax
import jax.numpy as jnp
from jax import lax
from jax.experimental import pallas as pl
from jax.experimental.pallas import tpu as pltpu

D_MODEL = 1024
N_Q_HEADS = 8
N_KV_HEADS = 2
HEAD_DIM = 128
GQA_GROUP = N_Q_HEADS // N_KV_HEADS
Q_DIM = N_Q_HEADS * HEAD_DIM
KV_DIM = N_KV_HEADS * HEAD_DIM
QKV_DIM = Q_DIM + 2 * KV_DIM
ROPE_HALF = HEAD_DIM // 2
ROPE_QUARTER = ROPE_HALF // 2
ROPE_THETA = 10000.0
GRID_W = 64
GMLP_WIDTH = D_MODEL
GMLP_GROUPS = 8
GMLP_GROUP_W = GMLP_WIDTH // GMLP_GROUPS
CHUNK = 128
D_FF = 2816
PLE_DIM = 256
EPS = 1e-6

BF16 = jnp.bfloat16
F32 = jnp.float32

VMEM_LIMIT_BYTES = 56 * 1024 * 1024
TOKEN_TILE = 512
ATTN_Q_TILE = 512
VT_ROWS = HEAD_DIM + 16
ROW_BLOCKS = 2
LOG2_E = 1.4426950408889634
MIN_SOFTMAX_SUM = 2.0 ** -60


def _dot(a, b):
    return jnp.dot(a, b, preferred_element_type=F32)


def _rms(x, g):
    return x * lax.rsqrt(jnp.mean(x * x, axis=-1, keepdims=True) + EPS) * g


def _gelu_exact(x):
    return 0.5 * x * (1.0 + lax.erf(x * (2.0 ** -0.5)))


def _ffn_stage(hs, norm_ref, wg_ref, wu_ref, wd_ref):
    xns = [_rms(h, norm_ref[...]).astype(BF16) for h in hs]
    gs = [_dot(xn, wg_ref[...]) for xn in xns]
    us = [_dot(xn, wu_ref[...]) for xn in xns]
    acts = [(g * jax.nn.sigmoid(g) * u).astype(BF16) for g, u in zip(gs, us)]
    return [h + 0.5 * _dot(a, wd_ref[...]) for h, a in zip(hs, acts)]


def _ple_stage(hs, norm_ref, wgate_ref, ps, wproj_ref):
    xns = [_rms(h, norm_ref[...]).astype(BF16) for h in hs]
    gates = [jax.nn.sigmoid(_dot(xn, wgate_ref[...])) for xn in xns]
    return [h + gate * _dot(p.astype(BF16), wproj_ref[...]) for h, gate, p in zip(hs, gates, ps)]


def _rope_partner(x):
    lane = lax.broadcasted_iota(jnp.int32, x.shape, 1)
    first = (lane % ROPE_HALF) < ROPE_QUARTER
    up = pltpu.roll(x, HEAD_DIM - ROPE_QUARTER, 1)
    down = pltpu.roll(x, ROPE_QUARTER, 1)
    return jnp.where(first, up, down)


def _qk_head(x, gain, cos, sin, scale):
    y = _rms(x, gain)
    y = y * cos + _rope_partner(y) * sin
    if scale is not None:
        y = y * scale
    return y.astype(BF16)


def _qkv_stage(h, rows, mixnorm_ref, wqkv_ref, qnorm_ref, knorm_ref, cos_ref, sin_ref,
               q_ref, k_ref, vt_ref):
    xn = _rms(h, mixnorm_ref[...]).astype(BF16)
    qkv = _dot(xn, wqkv_ref[...])
    cos = cos_ref[rows, :]
    sin = sin_ref[rows, :]
    for hd in range(N_Q_HEADS):
        sl = slice(hd * HEAD_DIM, (hd + 1) * HEAD_DIM)
        q_ref[rows, sl] = _qk_head(qkv[:, sl], qnorm_ref[...], cos, sin, HEAD_DIM ** -0.5 * LOG2_E)
    for hd in range(N_KV_HEADS):
        sl = slice(hd * HEAD_DIM, (hd + 1) * HEAD_DIM)
        src = slice(Q_DIM + hd * HEAD_DIM, Q_DIM + (hd + 1) * HEAD_DIM)
        k_ref[rows, sl] = _qk_head(qkv[:, src], knorm_ref[...], cos, sin, None)
    ones = jnp.ones((VT_ROWS - HEAD_DIM, h.shape[0]), BF16)
    for hd in range(N_KV_HEADS):
        src = slice(Q_DIM + KV_DIM + hd * HEAD_DIM, Q_DIM + KV_DIM + (hd + 1) * HEAD_DIM)
        vt_ref[0, 0, hd * VT_ROWS:hd * VT_ROWS + HEAD_DIM, rows] = qkv[:, src].T.astype(BF16)
        vt_ref[0, 0, hd * VT_ROWS + HEAD_DIM:(hd + 1) * VT_ROWS, rows] = ones


def _gmlp_stage(h, mixnorm_ref, win_ref, vnorm_ref, ws_ref, bs_ref, wout_ref):
    tm = h.shape[0]
    n_chunks = tm // CHUNK
    xn = _rms(h, mixnorm_ref[...]).astype(BF16)
    z = _gelu_exact(_dot(xn, win_ref[...]))
    u = z[:, :GMLP_WIDTH]
    v = _rms(z[:, GMLP_WIDTH:], vnorm_ref[...]).astype(BF16)
    cols = []
    for g in range(GMLP_GROUPS):
        gs = slice(g * GMLP_GROUP_W, (g + 1) * GMLP_GROUP_W)
        rhs = jnp.concatenate([v[c * CHUNK:(c + 1) * CHUNK, gs] for c in range(n_chunks)], axis=1)
        cols.append(_dot(ws_ref[g], rhs))
    rows = []
    for c in range(n_chunks):
        cs = slice(c * GMLP_GROUP_W, (c + 1) * GMLP_GROUP_W)
        rows.append(jnp.concatenate([cols[g][:, cs] for g in range(GMLP_GROUPS)], axis=1) + bs_ref[...])
    sv = jnp.concatenate(rows, axis=0)
    return h + _dot((u * sv).astype(BF16), wout_ref[...])


def _row_blocks(ref, n_blocks=ROW_BLOCKS):
    n = ref.shape[0] // n_blocks
    return [slice(i * n, (i + 1) * n) for i in range(n_blocks)]


def _pre_attn_kernel(h_ref, n1_ref, wg_ref, wu_ref, wd_ref, mixnorm_ref, wqkv_ref, qnorm_ref, knorm_ref,
                     cos_ref, sin_ref, hout_ref, q_ref, k_ref, vt_ref):
    blocks = _row_blocks(h_ref)
    hs = _ffn_stage([h_ref[rows, :] for rows in blocks], n1_ref, wg_ref, wu_ref, wd_ref)
    for rows, h in zip(blocks, hs):
        hout_ref[rows, :] = h
        _qkv_stage(h, rows, mixnorm_ref, wqkv_ref, qnorm_ref, knorm_ref, cos_ref, sin_ref, q_ref, k_ref, vt_ref)


def _nt_dot(a, b):
    return lax.dot_general(a, b, (((1,), (1,)), ((), ())), preferred_element_type=F32)


def _softmax_pv_bounded(q, k_ref, vt_ref, shift):
    n_chunks, tk = vt_ref.shape[1], vt_ref.shape[3]
    acc = None
    for j in range(n_chunks):
        st = _nt_dot(k_ref[0, j * tk:(j + 1) * tk, :], q)
        pv = _dot(vt_ref[0, j], jnp.exp2(st - shift).astype(BF16))
        acc = pv if acc is None else acc + pv
    return acc


def _softmax_pv_online(q, k_ref, vt_ref):
    n_chunks, tk = vt_ref.shape[1], vt_ref.shape[3]
    m = acc = None
    for j in range(n_chunks):
        s = _nt_dot(q, k_ref[0, j * tk:(j + 1) * tk, :])
        m_chunk = jnp.max(s, axis=-1, keepdims=True)
        m_new = m_chunk if m is None else jnp.maximum(m, m_chunk)
        pv = _nt_dot(jnp.exp2(s - m_new).astype(BF16), vt_ref[0, j])
        acc = pv if m is None else jnp.exp2(m - m_new) * acc + pv
        m = m_new
    return acc


def _attn_kernel(shift_ref, q_ref, k_ref, vt_ref, o_ref):
    tq = q_ref.shape[1]
    q = jnp.concatenate([q_ref[0, :, g * HEAD_DIM:(g + 1) * HEAD_DIM] for g in range(GQA_GROUP)], axis=0)

    def store(o):
        for g in range(GQA_GROUP):
            o_ref[0, :, g * HEAD_DIM:(g + 1) * HEAD_DIM] = o[g * tq:(g + 1) * tq].astype(o_ref.dtype)

    acc = _softmax_pv_bounded(q, k_ref, vt_ref, shift_ref[0, 0])
    sums = acc[HEAD_DIM:HEAD_DIM + 1]
    store((acc[:HEAD_DIM] / sums).T)

    @pl.when(jnp.logical_not(jnp.min(sums) >= MIN_SOFTMAX_SUM))
    def _():
        acc = _softmax_pv_online(q, k_ref, vt_ref)
        store(acc[:, :HEAD_DIM] / acc[:, HEAD_DIM:HEAD_DIM + 1])


def _post_attn_kernel(h_ref, o_ref, wo_ref, n2_ref, wg_ref, wu_ref, wd_ref, pn_ref, pgate_ref, p_ref, pproj_ref,
                      hout_ref):
    blocks = _row_blocks(h_ref)
    hs = [h_ref[rows, :] + _dot(o_ref[rows, :], wo_ref[...]) for rows in blocks]
    hs = _ffn_stage(hs, n2_ref, wg_ref, wu_ref, wd_ref)
    hs = _ple_stage(hs, pn_ref, pgate_ref, [p_ref[rows, :] for rows in blocks], pproj_ref)
    for rows, h in zip(blocks, hs):
        hout_ref[rows, :] = h


def _ffn_gmlp_kernel(h_ref, n1_ref, wg_ref, wu_ref, wd_ref, mixnorm_ref, win_ref, vnorm_ref, ws_ref, bs_ref,
                     wout_ref, hout_ref):
    blocks = _row_blocks(h_ref)
    hs = _ffn_stage([h_ref[rows, :] for rows in blocks], n1_ref, wg_ref, wu_ref, wd_ref)
    for rows, h in zip(blocks, hs):
        hout_ref[rows, :] = _gmlp_stage(h, mixnorm_ref, win_ref, vnorm_ref, ws_ref, bs_ref, wout_ref)


def _ffn_ple_kernel(h_ref, n2_ref, wg_ref, wu_ref, wd_ref, pn_ref, pgate_ref, p_ref, pproj_ref, hout_ref):
    blocks = _row_blocks(h_ref)
    hs = _ffn_stage([h_ref[rows, :] for rows in blocks], n2_ref, wg_ref, wu_ref, wd_ref)
    hs = _ple_stage(hs, pn_ref, pgate_ref, [p_ref[rows, :] for rows in blocks], pproj_ref)
    for rows, h in zip(blocks, hs):
        hout_ref[rows, :] = h


def _resident(arr):
    nd = arr.ndim
    return pl.BlockSpec(arr.shape, lambda *_: (0,) * nd, pipeline_mode=pl.Buffered(1))


def _rows(tm, width):
    return pl.BlockSpec((tm, width), lambda i: (i, 0))


def _layer_rows(layer, tm, width):
    return pl.BlockSpec((None, tm, width), lambda i: (layer, i, 0))


def _params(n_axes=1):
    return pltpu.CompilerParams(dimension_semantics=("parallel",) * n_axes,
                                vmem_limit_bytes=VMEM_LIMIT_BYTES)


def _token_tile(seq_len):
    tm = min(TOKEN_TILE, seq_len)
    assert seq_len % tm == 0 and tm % (CHUNK * ROW_BLOCKS) == 0
    return tm


def _pre_attn(h, seq_len, ffn, mixnorm, wqkv, qnorm, knorm, cos, sin):
    t = h.shape[0]
    tm = _token_tile(seq_len)
    tiles_per_seq = seq_len // tm
    weights = (*ffn, mixnorm, wqkv, qnorm, knorm)
    table = pl.BlockSpec((tm, HEAD_DIM), lambda i: (i % tiles_per_seq, 0))
    vt_spec = pl.BlockSpec((1, 1, N_KV_HEADS * VT_ROWS, tm), lambda i: (i // tiles_per_seq, i % tiles_per_seq, 0, 0))
    return pl.pallas_call(
        _pre_attn_kernel,
        grid=(t // tm,),
        in_specs=[_rows(tm, D_MODEL), *[_resident(w) for w in weights], table, table],
        out_specs=[_rows(tm, D_MODEL), _rows(tm, Q_DIM), _rows(tm, KV_DIM), vt_spec],
        out_shape=[jax.ShapeDtypeStruct((t, D_MODEL), F32), jax.ShapeDtypeStruct((t, Q_DIM), BF16),
                   jax.ShapeDtypeStruct((t, KV_DIM), BF16),
                   jax.ShapeDtypeStruct((t // seq_len, tiles_per_seq, N_KV_HEADS * VT_ROWS, tm), BF16)],
        compiler_params=_params(),
        name="pre_attn",
    )(h, *weights, cos, sin)


def _attention(q, k, vt, shift, batch, seq_len):
    tq = min(ATTN_Q_TILE, seq_len)
    assert seq_len % tq == 0
    gw = GQA_GROUP * HEAD_DIM
    q3 = q.reshape(batch, seq_len, Q_DIM)
    k3 = k.reshape(batch, seq_len, KV_DIM)
    qspec = pl.BlockSpec((1, tq, gw), lambda b, hk, i: (b, i, hk))
    kspec = pl.BlockSpec((1, seq_len, HEAD_DIM), lambda b, hk, i: (b, 0, hk))
    n_chunks, tk = vt.shape[1], vt.shape[3]
    vspec = pl.BlockSpec((1, n_chunks, VT_ROWS, tk), lambda b, hk, i: (b, 0, hk, 0))
    o = pl.pallas_call(
        _attn_kernel,
        grid=(batch, N_KV_HEADS, seq_len // tq),
        in_specs=[pl.BlockSpec(memory_space=pltpu.SMEM), qspec, kspec, vspec],
        out_specs=qspec,
        out_shape=jax.ShapeDtypeStruct((batch, seq_len, Q_DIM), BF16),
        compiler_params=_params(3),
        name="attention",
    )(shift, q3, k3, vt)
    return o.reshape(batch * seq_len, Q_DIM)


def _post_attn(h, o, wo, ffn, ple, p, layer, seq_len):
    t = h.shape[0]
    tm = _token_tile(seq_len)
    pn, pgate, pproj = ple
    return pl.pallas_call(
        _post_attn_kernel,
        grid=(t // tm,),
        in_specs=[_rows(tm, D_MODEL), _rows(tm, Q_DIM), _resident(wo), *[_resident(w) for w in ffn],
                  _resident(pn), _resident(pgate), _layer_rows(layer, tm, PLE_DIM), _resident(pproj)],
        out_specs=_rows(tm, D_MODEL),
        out_shape=jax.ShapeDtypeStruct((t, D_MODEL), F32),
        compiler_params=_params(),
        name="post_attn",
    )(h, o, wo, *ffn, pn, pgate, p, pproj)


def _ffn_gmlp(h, ffn, gmlp, seq_len):
    t = h.shape[0]
    tm = _token_tile(seq_len)
    weights = (*ffn, *gmlp)
    return pl.pallas_call(
        _ffn_gmlp_kernel,
        grid=(t // tm,),
        in_specs=[_rows(tm, D_MODEL), *[_resident(w) for w in weights]],
        out_specs=_rows(tm, D_MODEL),
        out_shape=jax.ShapeDtypeStruct((t, D_MODEL), F32),
        compiler_params=_params(),
        name="ffn_gmlp",
    )(h, *weights)


def _ffn_ple(h, ffn, ple, p, layer, seq_len):
    t = h.shape[0]
    tm = _token_tile(seq_len)
    pn, pgate, pproj = ple
    return pl.pallas_call(
        _ffn_ple_kernel,
        grid=(t // tm,),
        in_specs=[_rows(tm, D_MODEL), *[_resident(w) for w in ffn], _resident(pn), _resident(pgate),
                  _layer_rows(layer, tm, PLE_DIM), _resident(pproj)],
        out_specs=_rows(tm, D_MODEL),
        out_shape=jax.ShapeDtypeStruct((t, D_MODEL), F32),
        compiler_params=_params(),
        name="ffn_ple",
    )(h, *ffn, pn, pgate, p, pproj)


def _rope_tables(seq_len):
    t = jnp.arange(seq_len, dtype=jnp.int32)
    row = (t // GRID_W).astype(F32)
    col = (t % GRID_W).astype(F32)
    inv = ROPE_THETA ** (-jnp.arange(0, ROPE_HALF, 2, dtype=F32) / ROPE_HALF)
    ang_r = row[:, None] * inv
    ang_c = col[:, None] * inv
    cos = jnp.concatenate([jnp.cos(ang_r)] * 2 + [jnp.cos(ang_c)] * 2, axis=-1)
    sin = jnp.concatenate([-jnp.sin(ang_r), jnp.sin(ang_r), -jnp.sin(ang_c), jnp.sin(ang_c)], axis=-1)
    return cos, sin


def _score_bound(q_gain, k_gain):
    bound = jnp.max(jnp.abs(q_gain)) * jnp.max(jnp.abs(k_gain)) * (HEAD_DIM ** 0.5 * LOG2_E * (1.0 + 2.0 ** -7))
    return bound.astype(F32).reshape(1, 1)


def _row(vec):
    return vec.reshape(1, -1).astype(F32)


def _trunk(x, p, params):
    batch, seq_len, _ = x.shape
    t = batch * seq_len
    h = x.reshape(t, D_MODEL)
    p = p.reshape(p.shape[0], t, PLE_DIM)
    cos, sin = _rope_tables(seq_len)
    l0, l1 = params

    h, q, k, vt = _pre_attn(h, seq_len, l0["ffn1"], l0["mix_norm"], l0["w_qkv"], l0["q_norm"], l0["k_norm"], cos, sin)
    o = _attention(q, k, vt, l0["score_bound"], batch, seq_len)
    h = _post_attn(h, o, l0["w_o"], l0["ffn2"], l0["ple"], p, 0, seq_len)

    h = _ffn_gmlp(h, l1["ffn1"], l1["gmlp"], seq_len)
    h = _ffn_ple(h, l1["ffn2"], l1["ple"], p, 1, seq_len)
    return h.reshape(batch, seq_len, D_MODEL)


def kernel(x_prompt, x_sample, p_prompt, p_sample, ffn1_norm, ffn1_w_gate, ffn1_w_up, ffn1_w_down, mix_norm,
           attn_w_qkv, attn_q_norm, attn_k_norm, attn_w_o, gmlp_w_in, gmlp_v_norm, gmlp_w_s, gmlp_b_s, gmlp_w_out,
           ffn2_norm, ffn2_w_gate, ffn2_w_up, ffn2_w_down, ple_norm, ple_w_gate, ple_w_proj):
    def ffn(norm, wg, wu, wd, i):
        return (_row(norm[i]), wg[i].astype(BF16), wu[i].astype(BF16), wd[i].astype(BF16))

    def ple(i):
        return (_row(ple_norm[i]), ple_w_gate[i].astype(BF16), ple_w_proj[i].astype(BF16))

    bias = jnp.repeat(gmlp_b_s[0].T.astype(F32), GMLP_GROUP_W, axis=1)
    layer0 = dict(
        ffn1=ffn(ffn1_norm, ffn1_w_gate, ffn1_w_up, ffn1_w_down, 0),
        mix_norm=_row(mix_norm[0]), w_qkv=attn_w_qkv[0].astype(BF16),
        q_norm=_row(attn_q_norm[0]), k_norm=_row(attn_k_norm[0]), w_o=attn_w_o[0].astype(BF16),
        score_bound=_score_bound(attn_q_norm[0], attn_k_norm[0]),
        ffn2=ffn(ffn2_norm, ffn2_w_gate, ffn2_w_up, ffn2_w_down, 0), ple=ple(0))
    layer1 = dict(
        ffn1=ffn(ffn1_norm, ffn1_w_gate, ffn1_w_up, ffn1_w_down, 1),
        gmlp=(_row(mix_norm[1]), gmlp_w_in[0].astype(BF16), _row(gmlp_v_norm[0]), gmlp_w_s[0].astype(BF16), bias,
              gmlp_w_out[0].astype(BF16)),
        ffn2=ffn(ffn2_norm, ffn2_w_gate, ffn2_w_up, ffn2_w_down, 1), ple=ple(1))
    params = (layer0, layer1)
    return (_trunk(x_prompt, p_prompt, params), _trunk(x_sample, p_sample, params))
```

```python
import jax
import jax.numpy as jnp
from jax import lax
from jax.experimental import pallas as pl
from jax.experimental.pallas import tpu as pltpu

D_MODEL = 1024
N_Q_HEADS = 8
N_KV_HEADS = 2
HEAD_DIM = 128
GQA_GROUP = N_Q_HEADS // N_KV_HEADS
Q_DIM = N_Q_HEADS * HEAD_DIM
KV_DIM = N_KV_HEADS * HEAD_DIM
QKV_DIM = Q_DIM + 2 * KV_DIM
ROPE_HALF = HEAD_DIM // 2
ROPE_QUARTER = ROPE_HALF // 2
ROPE_THETA = 10000.0
GRID_W = 64
GMLP_WIDTH = D_MODEL
GMLP_GROUPS = 8
GMLP_GROUP_W = GMLP_WIDTH // GMLP_GROUPS
CHUNK = 128
D_FF = 2816
PLE_DIM = 256
EPS = 1e-6

BF16 = jnp.bfloat16
F32 = jnp.float32

VMEM_LIMIT_BYTES = 56 * 1024 * 1024
TOKEN_TILE = 512
ATTN_Q_TILE = 512
VT_ROWS = HEAD_DIM + 16
ROW_BLOCKS = 2
LOG2_E = 1.4426950408889634
MIN_SOFTMAX_SUM = 2.0 ** -60


def _dot(a, b):
    return jnp.dot(a, b, preferred_element_type=F32)


def _rms(x, g):
    return x * lax.rsqrt(jnp.mean(x * x, axis=-1, keepdims=True) + EPS) * g


def _gelu_exact(x):
    return 0.5 * x * (1.0 + lax.erf(x * (2.0 ** -0.5)))


def _ffn_stage(hs, norm_ref, wg_ref, wu_ref, wd_ref):
    xns = [_rms(h, norm_ref[...]).astype(BF16) for h in hs]
    gs = [_dot(xn, wg_ref[...]) for xn in xns]
    us = [_dot(xn, wu_ref[...]) for xn in xns]
    acts = [(g * jax.nn.sigmoid(g) * u).astype(BF16) for g, u in zip(gs, us)]
    return [h + 0.5 * _dot(a, wd_ref[...]) for h, a in zip(hs, acts)]


def _ple_stage(hs, norm_ref, wgate_ref, ps, wproj_ref):
    xns = [_rms(h, norm_ref[...]).astype(BF16) for h in hs]
    gates = [jax.nn.sigmoid(_dot(xn, wgate_ref[...])) for xn in xns]
    return [h + gate * _dot(p.astype(BF16), wproj_ref[...]) for h, gate, p in zip(hs, gates, ps)]


def _rope_partner(x):
    lane = lax.broadcasted_iota(jnp.int32, x.shape, 1)
    first = (lane % ROPE_HALF) < ROPE_QUARTER
    up = pltpu.roll(x, HEAD_DIM - ROPE_QUARTER, 1)
    down = pltpu.roll(x, ROPE_QUARTER, 1)
    return jnp.where(first, up, down)


def _qk_head(x, gain, cos, sin, scale):
    y = _rms(x, gain)
    y = y * cos + _rope_partner(y) * sin
    if scale is not None:
        y = y * scale
    return y.astype(BF16)


def _qkv_heads(qkv, rows, qnorm_ref, knorm_ref, cos_ref, sin_ref, q_ref, k_ref, vt_ref):
    cos = cos_ref[rows, :]
    sin = sin_ref[rows, :]
    for hd in range(N_Q_HEADS):
        sl = slice(hd * HEAD_DIM, (hd + 1) * HEAD_DIM)
        q_ref[rows, sl] = _qk_head(qkv[:, sl], qnorm_ref[...], cos, sin, HEAD_DIM ** -0.5 * LOG2_E)
    for hd in range(N_KV_HEADS):
        sl = slice(hd * HEAD_DIM, (hd + 1) * HEAD_DIM)
        src = slice(Q_DIM + hd * HEAD_DIM, Q_DIM + (hd + 1) * HEAD_DIM)
        k_ref[rows, sl] = _qk_head(qkv[:, src], knorm_ref[...], cos, sin, None)
    ones = jnp.ones((VT_ROWS - HEAD_DIM, qkv.shape[0]), BF16)
    for hd in range(N_KV_HEADS):
        src = slice(Q_DIM + KV_DIM + hd * HEAD_DIM, Q_DIM + KV_DIM + (hd + 1) * HEAD_DIM)
        vt_ref[0, 0, hd * VT_ROWS:hd * VT_ROWS + HEAD_DIM, rows] = qkv[:, src].T.astype(BF16)
        vt_ref[0, 0, hd * VT_ROWS + HEAD_DIM:(hd + 1) * VT_ROWS, rows] = ones


def _spatial_gate(v, ws_ref, bs_ref):
    n_chunks = v.shape[0] // CHUNK
    cols = []
    for g in range(GMLP_GROUPS):
        gs = slice(g * GMLP_GROUP_W, (g + 1) * GMLP_GROUP_W)
        rhs = jnp.concatenate([v[c * CHUNK:(c + 1) * CHUNK, gs] for c in range(n_chunks)], axis=1)
        cols.append(_dot(ws_ref[g], rhs))
    rows = []
    for c in range(n_chunks):
        cs = slice(c * GMLP_GROUP_W, (c + 1) * GMLP_GROUP_W)
        rows.append(jnp.concatenate([cols[g][:, cs] for g in range(GMLP_GROUPS)], axis=1) + bs_ref[...])
    return jnp.concatenate(rows, axis=0)


def _gmlp_stage(hs, mixnorm_ref, win_ref, vnorm_ref, ws_ref, bs_ref, wout_ref):
    xns = [_rms(h, mixnorm_ref[...]).astype(BF16) for h in hs]
    zs = [_gelu_exact(_dot(xn, win_ref[...])) for xn in xns]
    vs = [_rms(z[:, GMLP_WIDTH:], vnorm_ref[...]).astype(BF16) for z in zs]
    svs = [_spatial_gate(v, ws_ref, bs_ref) for v in vs]
    gated = [(z[:, :GMLP_WIDTH] * sv).astype(BF16) for z, sv in zip(zs, svs)]
    return [h + _dot(x, wout_ref[...]) for h, x in zip(hs, gated)]


def _row_blocks(ref, n_blocks=ROW_BLOCKS):
    n = ref.shape[0] // n_blocks
    return [slice(i * n, (i + 1) * n) for i in range(n_blocks)]


def _pre_attn_kernel(h_ref, n1_ref, wg_ref, wu_ref, wd_ref, mixnorm_ref, wqkv_ref, qnorm_ref, knorm_ref,
                     cos_ref, sin_ref, hout_ref, q_ref, k_ref, vt_ref, qkv_ref):
    @pl.when(pl.program_id(0) == 0)
    def _():
        qkv_ref[...] = jnp.zeros(qkv_ref.shape, qkv_ref.dtype)

    blocks = _row_blocks(h_ref)
    hs = _ffn_stage([h_ref[rows, :] for rows in blocks], n1_ref, wg_ref, wu_ref, wd_ref)
    for rows in blocks:
        _qkv_heads(qkv_ref[rows, :], rows, qnorm_ref, knorm_ref, cos_ref, sin_ref, q_ref, k_ref, vt_ref)
    xns = [_rms(h, mixnorm_ref[...]).astype(BF16) for h in hs]
    for rows, h, xn in zip(blocks, hs, xns):
        hout_ref[rows, :] = h
        qkv_ref[rows, :] = _dot(xn, wqkv_ref[...])


def _nt_dot(a, b):
    return lax.dot_general(a, b, (((1,), (1,)), ((), ())), preferred_element_type=F32)


def _softmax_pv_bounded(q, k_ref, vt_ref, shift):
    n_chunks, tk = vt_ref.shape[1], vt_ref.shape[3]
    acc = None
    for j in range(n_chunks):
        st = _nt_dot(k_ref[0, j * tk:(j + 1) * tk, :], q)
        pv = _dot(vt_ref[0, j], jnp.exp2(st - shift).astype(BF16))
        acc = pv if acc is None else acc + pv
    return acc


def _softmax_pv_online(q, k_ref, vt_ref):
    n_chunks, tk = vt_ref.shape[1], vt_ref.shape[3]
    m = acc = None
    for j in range(n_chunks):
        s = _nt_dot(q, k_ref[0, j * tk:(j + 1) * tk, :])
        m_chunk = jnp.max(s, axis=-1, keepdims=True)
        m_new = m_chunk if m is None else jnp.maximum(m, m_chunk)
        pv = _nt_dot(jnp.exp2(s - m_new).astype(BF16), vt_ref[0, j])
        acc = pv if m is None else jnp.exp2(m - m_new) * acc + pv
        m = m_new
    return acc


def _attn_kernel(shift_ref, q_ref, k_ref, vt_ref, o_ref):
    tq = q_ref.shape[1]
    q = jnp.concatenate([q_ref[0, :, g * HEAD_DIM:(g + 1) * HEAD_DIM] for g in range(GQA_GROUP)], axis=0)

    def store(o):
        for g in range(GQA_GROUP):
            o_ref[0, :, g * HEAD_DIM:(g + 1) * HEAD_DIM] = o[g * tq:(g + 1) * tq].astype(o_ref.dtype)

    acc = _softmax_pv_bounded(q, k_ref, vt_ref, shift_ref[0, 0])
    sums = acc[HEAD_DIM:HEAD_DIM + 1]
    store((acc[:HEAD_DIM] / sums).T)

    @pl.when(jnp.logical_not(jnp.min(sums) >= MIN_SOFTMAX_SUM))
    def _():
        acc = _softmax_pv_online(q, k_ref, vt_ref)
        store(acc[:, :HEAD_DIM] / acc[:, HEAD_DIM:HEAD_DIM + 1])


def _post_attn_kernel(h_ref, o_ref, wo_ref, n2_ref, wg_ref, wu_ref, wd_ref, pn_ref, pgate_ref, p_ref, pproj_ref,
                      hout_ref):
    blocks = _row_blocks(h_ref)
    hs = [h_ref[rows, :] + _dot(o_ref[rows, :], wo_ref[...]) for rows in blocks]
    hs = _ffn_stage(hs, n2_ref, wg_ref, wu_ref, wd_ref)
    hs = _ple_stage(hs, pn_ref, pgate_ref, [p_ref[rows, :] for rows in blocks], pproj_ref)
    for rows, h in zip(blocks, hs):
        hout_ref[rows, :] = h


def _ffn_gmlp_kernel(h_ref, n1_ref, wg_ref, wu_ref, wd_ref, mixnorm_ref, win_ref, vnorm_ref, ws_ref, bs_ref,
                     wout_ref, hout_ref):
    blocks = _row_blocks(h_ref)
    hs = _ffn_stage([h_ref[rows, :] for rows in blocks], n1_ref, wg_ref, wu_ref, wd_ref)
    hs = _gmlp_stage(hs, mixnorm_ref, win_ref, vnorm_ref, ws_ref, bs_ref, wout_ref)
    for rows, h in zip(blocks, hs):
        hout_ref[rows, :] = h


def _ffn_ple_kernel(h_ref, n2_ref, wg_ref, wu_ref, wd_ref, pn_ref, pgate_ref, p_ref, pproj_ref, hout_ref):
    blocks = _row_blocks(h_ref)
    hs = _ffn_stage([h_ref[rows, :] for rows in blocks], n2_ref, wg_ref, wu_ref, wd_ref)
    hs = _ple_stage(hs, pn_ref, pgate_ref, [p_ref[rows, :] for rows in blocks], pproj_ref)
    for rows, h in zip(blocks, hs):
        hout_ref[rows, :] = h


def _resident(arr):
    nd = arr.ndim
    return pl.BlockSpec(arr.shape, lambda *_: (0,) * nd, pipeline_mode=pl.Buffered(1))


def _rows(tm, width):
    return pl.BlockSpec((tm, width), lambda i: (i, 0))


def _layer_rows(layer, tm, width):
    return pl.BlockSpec((None, tm, width), lambda i: (layer, i, 0))


def _params(n_axes=1):
    return pltpu.CompilerParams(dimension_semantics=("parallel",) * n_axes,
                                vmem_limit_bytes=VMEM_LIMIT_BYTES)


def _token_tile(seq_len):
    tm = min(TOKEN_TILE, seq_len)
    assert seq_len % tm == 0 and tm % (CHUNK * ROW_BLOCKS) == 0
    return tm


def _pre_attn(h, seq_len, ffn, mixnorm, wqkv, qnorm, knorm, cos, sin):
    t = h.shape[0]
    tm = _token_tile(seq_len)
    tiles_per_seq = seq_len // tm
    n_tiles = t // tm
    weights = (*ffn, mixnorm, wqkv, qnorm, knorm)

    def cur(i):
        return jnp.minimum(i, n_tiles - 1)

    def prev(i):
        return jnp.maximum(i - 1, 0)

    table = pl.BlockSpec((tm, HEAD_DIM), lambda i: (prev(i) % tiles_per_seq, 0))
    vt_spec = pl.BlockSpec((1, 1, N_KV_HEADS * VT_ROWS, tm),
                           lambda i: (prev(i) // tiles_per_seq, prev(i) % tiles_per_seq, 0, 0))
    return pl.pallas_call(
        _pre_attn_kernel,
        grid=(n_tiles + 1,),
        in_specs=[pl.BlockSpec((tm, D_MODEL), lambda i: (cur(i), 0)), *[_resident(w) for w in weights], table, table],
        out_specs=[pl.BlockSpec((tm, D_MODEL), lambda i: (cur(i), 0)),
                   pl.BlockSpec((tm, Q_DIM), lambda i: (prev(i), 0)),
                   pl.BlockSpec((tm, KV_DIM), lambda i: (prev(i), 0)), vt_spec],
        out_shape=[jax.ShapeDtypeStruct((t, D_MODEL), F32), jax.ShapeDtypeStruct((t, Q_DIM), BF16),
                   jax.ShapeDtypeStruct((t, KV_DIM), BF16),
                   jax.ShapeDtypeStruct((t // seq_len, tiles_per_seq, N_KV_HEADS * VT_ROWS, tm), BF16)],
        scratch_shapes=[pltpu.VMEM((tm, QKV_DIM), F32)],
        compiler_params=pltpu.CompilerParams(dimension_semantics=("arbitrary",), vmem_limit_bytes=VMEM_LIMIT_BYTES),
        name="pre_attn",
    )(h, *weights, cos, sin)


def _attention(q, k, vt, shift, batch, seq_len):
    tq = min(ATTN_Q_TILE, seq_len)
    assert seq_len % tq == 0
    gw = GQA_GROUP * HEAD_DIM
    q3 = q.reshape(batch, seq_len, Q_DIM)
    k3 = k.reshape(batch, seq_len, KV_DIM)
    qspec = pl.BlockSpec((1, tq, gw), lambda b, hk, i: (b, i, hk))
    kspec = pl.BlockSpec((1, seq_len, HEAD_DIM), lambda b, hk, i: (b, 0, hk))
    n_chunks, tk = vt.shape[1], vt.shape[3]
    vspec = pl.BlockSpec((1, n_chunks, VT_ROWS, tk), lambda b, hk, i: (b, 0, hk, 0))
    o = pl.pallas_call(
        _attn_kernel,
        grid=(batch, N_KV_HEADS, seq_len // tq),
        in_specs=[pl.BlockSpec(memory_space=pltpu.SMEM), qspec, kspec, vspec],
        out_specs=qspec,
        out_shape=jax.ShapeDtypeStruct((batch, seq_len, Q_DIM), BF16),
        compiler_params=_params(3),
        name="attention",
    )(shift, q3, k3, vt)
    return o.reshape(batch * seq_len, Q_DIM)


def _post_attn(h, o, wo, ffn, ple, p, layer, seq_len):
    t = h.shape[0]
    tm = _token_tile(seq_len)
    pn, pgate, pproj = ple
    return pl.pallas_call(
        _post_attn_kernel,
        grid=(t // tm,),
        in_specs=[_rows(tm, D_MODEL), _rows(tm, Q_DIM), _resident(wo), *[_resident(w) for w in ffn],
                  _resident(pn), _resident(pgate), _layer_rows(layer, tm, PLE_DIM), _resident(pproj)],
        out_specs=_rows(tm, D_MODEL),
        out_shape=jax.ShapeDtypeStruct((t, D_MODEL), F32),
        compiler_params=_params(),
        name="post_attn",
    )(h, o, wo, *ffn, pn, pgate, p, pproj)


def _ffn_gmlp(h, ffn, gmlp, seq_len):
    t = h.shape[0]
    tm = _token_tile(seq_len)
    weights = (*ffn, *gmlp)
    return pl.pallas_call(
        _ffn_gmlp_kernel,
        grid=(t // tm,),
        in_specs=[_rows(tm, D_MODEL), *[_resident(w) for w in weights]],
        out_specs=_rows(tm, D_MODEL),
        out_shape=jax.ShapeDtypeStruct((t, D_MODEL), F32),
        compiler_params=_params(),
        name="ffn_gmlp",
    )(h, *weights)


def _ffn_ple(h, ffn, ple, p, layer, seq_len):
    t = h.shape[0]
    tm = _token_tile(seq_len)
    pn, pgate, pproj = ple
    return pl.pallas_call(
        _ffn_ple_kernel,
        grid=(t // tm,),
        in_specs=[_rows(tm, D_MODEL), *[_resident(w) for w in ffn], _resident(pn), _resident(pgate),
                  _layer_rows(layer, tm, PLE_DIM), _resident(pproj)],
        out_specs=_rows(tm, D_MODEL),
        out_shape=jax.ShapeDtypeStruct((t, D_MODEL), F32),
        compiler_params=_params(),
        name="ffn_ple",
    )(h, *ffn, pn, pgate, p, pproj)


def _rope_tables(seq_len):
    t = jnp.arange(seq_len, dtype=jnp.int32)
    row = (t // GRID_W).astype(F32)
    col = (t % GRID_W).astype(F32)
    inv = ROPE_THETA ** (-jnp.arange(0, ROPE_HALF, 2, dtype=F32) / ROPE_HALF)
    ang_r = row[:, None] * inv
    ang_c = col[:, None] * inv
    cos = jnp.concatenate([jnp.cos(ang_r)] * 2 + [jnp.cos(ang_c)] * 2, axis=-1)
    sin = jnp.concatenate([-jnp.sin(ang_r), jnp.sin(ang_r), -jnp.sin(ang_c), jnp.sin(ang_c)], axis=-1)
    return cos, sin


def _score_bound(q_gain, k_gain):
    bound = jnp.max(jnp.abs(q_gain)) * jnp.max(jnp.abs(k_gain)) * (HEAD_DIM ** 0.5 * LOG2_E * (1.0 + 2.0 ** -7))
    return bound.astype(F32).reshape(1, 1)


def _row(vec):
    return vec.reshape(1, -1).astype(F32)


def _trunk(x, p, params):
    batch, seq_len, _ = x.shape
    t = batch * seq_len
    h = x.reshape(t, D_MODEL)
    p = p.reshape(p.shape[0], t, PLE_DIM)
    cos, sin = _rope_tables(seq_len)
    l0, l1 = params

    h, q, k, vt = _pre_attn(h, seq_len, l0["ffn1"], l0["mix_norm"], l0["w_qkv"], l0["q_norm"], l0["k_norm"], cos, sin)
    o = _attention(q, k, vt, l0["score_bound"], batch, seq_len)
    h = _post_attn(h, o, l0["w_o"], l0["ffn2"], l0["ple"], p, 0, seq_len)

    h = _ffn_gmlp(h, l1["ffn1"], l1["gmlp"], seq_len)
    h = _ffn_ple(h, l1["ffn2"], l1["ple"], p, 1, seq_len)
    return h.reshape(batch, seq_len, D_MODEL)


def kernel(x_prompt, x_sample, p_prompt, p_sample, ffn1_norm, ffn1_w_gate, ffn1_w_up, ffn1_w_down, mix_norm,
           attn_w_qkv, attn_q_norm, attn_k_norm, attn_w_o, gmlp_w_in, gmlp_v_norm, gmlp_w_s, gmlp_b_s, gmlp_w_out,
           ffn2_norm, ffn2_w_gate, ffn2_w_up, ffn2_w_down, ple_norm, ple_w_gate, ple_w_proj):
    def ffn(norm, wg, wu, wd, i):
        return (_row(norm[i]), wg[i].astype(BF16), wu[i].astype(BF16), wd[i].astype(BF16))

    def ple(i):
        return (_row(ple_norm[i]), ple_w_gate[i].astype(BF16), ple_w_proj[i].astype(BF16))

    bias = jnp.repeat(gmlp_b_s[0].T.astype(F32), GMLP_GROUP_W, axis=1)
    layer0 = dict(
        ffn1=ffn(ffn1_norm, ffn1_w_gate, ffn1_w_up, ffn1_w_down, 0),
        mix_norm=_row(mix_norm[0]), w_qkv=attn_w_qkv[0].astype(BF16),
        q_norm=_row(attn_q_norm[0]), k_norm=_row(attn_k_norm[0]), w_o=attn_w_o[0].astype(BF16),
        score_bound=_score_bound(attn_q_norm[0], attn_k_norm[0]),
        ffn2=ffn(ffn2_norm, ffn2_w_gate, ffn2_w_up, ffn2_w_down, 0), ple=ple(0))
    layer1 = dict(
        ffn1=ffn(ffn1_norm, ffn1_w_gate, ffn1_w_up, ffn1_w_down, 1),
        gmlp=(_row(mix_norm[1]), gmlp_w_in[0].astype(BF16), _row(gmlp_v_norm[0]), gmlp_w_s[0].astype(BF16), bias,
              gmlp_w_out[0].astype(BF16)),
        ffn2=ffn(ffn2_norm, ffn2_w_gate, ffn2_w_up, ffn2_w_down, 1), ple=ple(1))
    params = (layer0, layer1)
    return (_trunk(x_prompt, p_prompt, params), _trunk(x_sample, p_sample, params))
```

```python
import jax
import jax.numpy as jnp
from jax import lax
from jax.experimental import pallas as pl
from jax.experimental.pallas import tpu as pltpu

D_MODEL = 1024
N_Q_HEADS = 8
N_KV_HEADS = 2
HEAD_DIM = 128
GQA_GROUP = N_Q_HEADS // N_KV_HEADS
Q_DIM = N_Q_HEADS * HEAD_DIM
KV_DIM = N_KV_HEADS * HEAD_DIM
QKV_DIM = Q_DIM + 2 * KV_DIM
ROPE_HALF = HEAD_DIM // 2
ROPE_QUARTER = ROPE_HALF // 2
ROPE_THETA = 10000.0
GRID_W = 64
GMLP_WIDTH = D_MODEL
GMLP_GROUPS = 8
GMLP_GROUP_W = GMLP_WIDTH // GMLP_GROUPS
CHUNK = 128
D_FF = 2816
PLE_DIM = 256
EPS = 1e-6

BF16 = jnp.bfloat16
F32 = jnp.float32

VMEM_LIMIT_BYTES = 56 * 1024 * 1024
TOKEN_TILE = 512
ATTN_Q_TILE = 512
VT_ROWS = HEAD_DIM + 16
ROW_BLOCKS = 2
LOG2_E = 1.4426950408889634
MIN_SOFTMAX_SUM = 2.0 ** -60


def _dot(a, b):
    return jnp.dot(a, b, preferred_element_type=F32)


def _rms(x, g):
    return x * lax.rsqrt(jnp.mean(x * x, axis=-1, keepdims=True) + EPS) * g


def _gelu_exact(x):
    return 0.5 * x * (1.0 + lax.erf(x * (2.0 ** -0.5)))


def _ffn_stage(hs, norm_ref, wg_ref, wu_ref, wd_ref):
    xns = [_rms(h, norm_ref[...]).astype(BF16) for h in hs]
    gs = [_dot(xn, wg_ref[...]) for xn in xns]
    us = [_dot(xn, wu_ref[...]) for xn in xns]
    acts = [(g * jax.nn.sigmoid(g) * u).astype(BF16) for g, u in zip(gs, us)]
    return [h + 0.5 * _dot(a, wd_ref[...]) for h, a in zip(hs, acts)]


def _ple_stage(hs, norm_ref, wgate_ref, ps, wproj_ref):
    xns = [_rms(h, norm_ref[...]).astype(BF16) for h in hs]
    gates = [jax.nn.sigmoid(_dot(xn, wgate_ref[...])) for xn in xns]
    return [h + gate * _dot(p.astype(BF16), wproj_ref[...]) for h, gate, p in zip(hs, gates, ps)]


def _rope_partner(x):
    lane = lax.broadcasted_iota(jnp.int32, x.shape, 1)
    first = (lane % ROPE_HALF) < ROPE_QUARTER
    up = pltpu.roll(x, HEAD_DIM - ROPE_QUARTER, 1)
    down = pltpu.roll(x, ROPE_QUARTER, 1)
    return jnp.where(first, up, down)


def _qk_head(x, gain, cos, sin, scale):
    y = _rms(x, gain)
    y = y * cos + _rope_partner(y) * sin
    if scale is not None:
        y = y * scale
    return y.astype(BF16)


def _qkv_heads(qkv, rows, qnorm_ref, knorm_ref, cos_ref, sin_ref, q_ref, k_ref, vt_ref):
    cos = cos_ref[rows, :]
    sin = sin_ref[rows, :]
    for hd in range(N_Q_HEADS):
        sl = slice(hd * HEAD_DIM, (hd + 1) * HEAD_DIM)
        q_ref[rows, sl] = _qk_head(qkv[:, sl], qnorm_ref[...], cos, sin, HEAD_DIM ** -0.5 * LOG2_E)
    for hd in range(N_KV_HEADS):
        sl = slice(hd * HEAD_DIM, (hd + 1) * HEAD_DIM)
        src = slice(Q_DIM + hd * HEAD_DIM, Q_DIM + (hd + 1) * HEAD_DIM)
        k_ref[rows, sl] = _qk_head(qkv[:, src], knorm_ref[...], cos, sin, None)
    ones = jnp.ones((VT_ROWS - HEAD_DIM, qkv.shape[0]), BF16)
    for hd in range(N_KV_HEADS):
        src = slice(Q_DIM + KV_DIM + hd * HEAD_DIM, Q_DIM + KV_DIM + (hd + 1) * HEAD_DIM)
        vt_ref[0, 0, hd * VT_ROWS:hd * VT_ROWS + HEAD_DIM, rows] = qkv[:, src].T.astype(BF16)
        vt_ref[0, 0, hd * VT_ROWS + HEAD_DIM:(hd + 1) * VT_ROWS, rows] = ones


def _spatial_gate(v, ws_ref, bs_ref):
    n_chunks = v.shape[0] // CHUNK
    cols = []
    for g in range(GMLP_GROUPS):
        gs = slice(g * GMLP_GROUP_W, (g + 1) * GMLP_GROUP_W)
        rhs = jnp.concatenate([v[c * CHUNK:(c + 1) * CHUNK, gs] for c in range(n_chunks)], axis=1)
        cols.append(_dot(ws_ref[g], rhs))
    rows = []
    for c in range(n_chunks):
        cs = slice(c * GMLP_GROUP_W, (c + 1) * GMLP_GROUP_W)
        rows.append(jnp.concatenate([cols[g][:, cs] for g in range(GMLP_GROUPS)], axis=1) + bs_ref[...])
    return jnp.concatenate(rows, axis=0)


def _gmlp_stage(hs, mixnorm_ref, win_ref, vnorm_ref, ws_ref, bs_ref, wout_ref):
    xns = [_rms(h, mixnorm_ref[...]).astype(BF16) for h in hs]
    zs = [_gelu_exact(_dot(xn, win_ref[...])) for xn in xns]
    vs = [_rms(z[:, GMLP_WIDTH:], vnorm_ref[...]).astype(BF16) for z in zs]
    svs = [_spatial_gate(v, ws_ref, bs_ref) for v in vs]
    gated = [(z[:, :GMLP_WIDTH] * sv).astype(BF16) for z, sv in zip(zs, svs)]
    return [h + _dot(x, wout_ref[...]) for h, x in zip(hs, gated)]


def _row_blocks(ref, n_blocks=ROW_BLOCKS):
    n = ref.shape[0] // n_blocks
    return [slice(i * n, (i + 1) * n) for i in range(n_blocks)]


def _pre_attn_kernel(h_ref, n1_ref, wg_ref, wu_ref, wd_ref, mixnorm_ref, wqkv_ref, qnorm_ref, knorm_ref,
                     cos_ref, sin_ref, hout_ref, q_ref, k_ref, vt_ref, qkv_ref):
    @pl.when(pl.program_id(0) == 0)
    def _():
        qkv_ref[...] = jnp.zeros(qkv_ref.shape, qkv_ref.dtype)

    blocks = _row_blocks(h_ref)
    hs = _ffn_stage([h_ref[rows, :] for rows in blocks], n1_ref, wg_ref, wu_ref, wd_ref)
    for rows in blocks:
        _qkv_heads(qkv_ref[rows, :], rows, qnorm_ref, knorm_ref, cos_ref, sin_ref, q_ref, k_ref, vt_ref)
    xns = [_rms(h, mixnorm_ref[...]).astype(BF16) for h in hs]
    for rows, h, xn in zip(blocks, hs, xns):
        hout_ref[rows, :] = h
        qkv_ref[rows, :] = _dot(xn, wqkv_ref[...])


def _nt_dot(a, b):
    return lax.dot_general(a, b, (((1,), (1,)), ((), ())), preferred_element_type=F32)


def _softmax_pv_bounded(q, k_ref, vt_ref, shift):
    n_chunks, tk = vt_ref.shape[1], vt_ref.shape[3]
    acc = None
    for j in range(n_chunks):
        st = _nt_dot(k_ref[0, j * tk:(j + 1) * tk, :], q)
        pv = _dot(vt_ref[0, j], jnp.exp2(st - shift).astype(BF16))
        acc = pv if acc is None else acc + pv
    return acc


def _softmax_pv_online(q, k_ref, vt_ref):
    n_chunks, tk = vt_ref.shape[1], vt_ref.shape[3]
    m = acc = None
    for j in range(n_chunks):
        s = _nt_dot(q, k_ref[0, j * tk:(j + 1) * tk, :])
        m_chunk = jnp.max(s, axis=-1, keepdims=True)
        m_new = m_chunk if m is None else jnp.maximum(m, m_chunk)
        pv = _nt_dot(jnp.exp2(s - m_new).astype(BF16), vt_ref[0, j])
        acc = pv if m is None else jnp.exp2(m - m_new) * acc + pv
        m = m_new
    return acc


def _attn_kernel(shift_ref, q_ref, k_ref, vt_ref, o_ref):
    tq = q_ref.shape[1]
    q = jnp.concatenate([q_ref[0, :, g * HEAD_DIM:(g + 1) * HEAD_DIM] for g in range(GQA_GROUP)], axis=0)

    def store(o):
        for g in range(GQA_GROUP):
            o_ref[0, :, g * HEAD_DIM:(g + 1) * HEAD_DIM] = o[g * tq:(g + 1) * tq].astype(o_ref.dtype)

    acc = _softmax_pv_bounded(q, k_ref, vt_ref, shift_ref[0, 0])
    sums = acc[HEAD_DIM:HEAD_DIM + 1]
    store((acc[:HEAD_DIM] / sums).T)

    @pl.when(jnp.logical_not(jnp.min(sums) >= MIN_SOFTMAX_SUM))
    def _():
        acc = _softmax_pv_online(q, k_ref, vt_ref)
        store(acc[:, :HEAD_DIM] / acc[:, HEAD_DIM:HEAD_DIM + 1])


def _post_attn_kernel(h_ref, o_ref, wo_ref, n2_ref, wg_ref, wu_ref, wd_ref, pn_ref, pgate_ref, p_ref, pproj_ref,
                      hout_ref):
    blocks = _row_blocks(h_ref)
    hs = [h_ref[rows, :] + _dot(o_ref[rows, :], wo_ref[...]) for rows in blocks]
    hs = _ffn_stage(hs, n2_ref, wg_ref, wu_ref, wd_ref)
    hs = _ple_stage(hs, pn_ref, pgate_ref, [p_ref[rows, :] for rows in blocks], pproj_ref)
    for rows, h in zip(blocks, hs):
        hout_ref[rows, :] = h


def _ffn_gmlp_kernel(h_ref, n1_ref, wg_ref, wu_ref, wd_ref, mixnorm_ref, win_ref, vnorm_ref, ws_ref, bs_ref,
                     wout_ref, hout_ref):
    blocks = _row_blocks(h_ref)
    hs = _ffn_stage([h_ref[rows, :] for rows in blocks], n1_ref, wg_ref, wu_ref, wd_ref)
    hs = _gmlp_stage(hs, mixnorm_ref, win_ref, vnorm_ref, ws_ref, bs_ref, wout_ref)
    for rows, h in zip(blocks, hs):
        hout_ref[rows, :] = h


def _ffn_ple_kernel(h_ref, n2_ref, wg_ref, wu_ref, wd_ref, pn_ref, pgate_ref, p_ref, pproj_ref, hout_ref):
    blocks = _row_blocks(h_ref)
    hs = _ffn_stage([h_ref[rows, :] for rows in blocks], n2_ref, wg_ref, wu_ref, wd_ref)
    hs = _ple_stage(hs, pn_ref, pgate_ref, [p_ref[rows, :] for rows in blocks], pproj_ref)
    for rows, h in zip(blocks, hs):
        hout_ref[rows, :] = h


def _resident(arr):
    nd = arr.ndim
    return pl.BlockSpec(arr.shape, lambda *_: (0,) * nd, pipeline_mode=pl.Buffered(1))


def _rows(tm, width):
    return pl.BlockSpec((tm, width), lambda i: (i, 0))


def _layer_rows(layer, tm, width):
    return pl.BlockSpec((None, tm, width), lambda i: (layer, i, 0))


def _params(n_axes=1):
    return pltpu.CompilerParams(dimension_semantics=("arbitrary",) * n_axes,
                                vmem_limit_bytes=VMEM_LIMIT_BYTES)


def _token_tile(seq_len):
    tm = min(TOKEN_TILE, seq_len)
    assert seq_len % tm == 0 and tm % (CHUNK * ROW_BLOCKS) == 0
    return tm


def _pre_attn(h, seq_len, ffn, mixnorm, wqkv, qnorm, knorm, cos, sin):
    t = h.shape[0]
    tm = _token_tile(seq_len)
    tiles_per_seq = seq_len // tm
    n_tiles = t // tm
    weights = (*ffn, mixnorm, wqkv, qnorm, knorm)

    def cur(i):
        return jnp.minimum(i, n_tiles - 1)

    def prev(i):
        return jnp.maximum(i - 1, 0)

    table = pl.BlockSpec((tm, HEAD_DIM), lambda i: (prev(i) % tiles_per_seq, 0))
    vt_spec = pl.BlockSpec((1, 1, N_KV_HEADS * VT_ROWS, tm),
                           lambda i: (prev(i) // tiles_per_seq, prev(i) % tiles_per_seq, 0, 0))
    return pl.pallas_call(
        _pre_attn_kernel,
        grid=(n_tiles + 1,),
        in_specs=[pl.BlockSpec((tm, D_MODEL), lambda i: (cur(i), 0)), *[_resident(w) for w in weights], table, table],
        out_specs=[pl.BlockSpec((tm, D_MODEL), lambda i: (cur(i), 0)),
                   pl.BlockSpec((tm, Q_DIM), lambda i: (prev(i), 0)),
                   pl.BlockSpec((tm, KV_DIM), lambda i: (prev(i), 0)), vt_spec],
        out_shape=[jax.ShapeDtypeStruct((t, D_MODEL), F32), jax.ShapeDtypeStruct((t, Q_DIM), BF16),
                   jax.ShapeDtypeStruct((t, KV_DIM), BF16),
                   jax.ShapeDtypeStruct((t // seq_len, tiles_per_seq, N_KV_HEADS * VT_ROWS, tm), BF16)],
        scratch_shapes=[pltpu.VMEM((tm, QKV_DIM), F32)],
        compiler_params=_params(),
        name="pre_attn",
    )(h, *weights, cos, sin)


def _attention(q, k, vt, shift, batch, seq_len):
    tq = min(ATTN_Q_TILE, seq_len)
    assert seq_len % tq == 0
    gw = GQA_GROUP * HEAD_DIM
    q3 = q.reshape(batch, seq_len, Q_DIM)
    k3 = k.reshape(batch, seq_len, KV_DIM)
    qspec = pl.BlockSpec((1, tq, gw), lambda b, hk, i: (b, i, hk))
    kspec = pl.BlockSpec((1, seq_len, HEAD_DIM), lambda b, hk, i: (b, 0, hk))
    n_chunks, tk = vt.shape[1], vt.shape[3]
    vspec = pl.BlockSpec((1, n_chunks, VT_ROWS, tk), lambda b, hk, i: (b, 0, hk, 0))
    o = pl.pallas_call(
        _attn_kernel,
        grid=(batch, N_KV_HEADS, seq_len // tq),
        in_specs=[pl.BlockSpec(memory_space=pltpu.SMEM), qspec, kspec, vspec],
        out_specs=qspec,
        out_shape=jax.ShapeDtypeStruct((batch, seq_len, Q_DIM), BF16),
        compiler_params=_params(3),
        name="attention",
    )(shift, q3, k3, vt)
    return o.reshape(batch * seq_len, Q_DIM)


def _post_attn(h, o, wo, ffn, ple, p, layer, seq_len):
    t = h.shape[0]
    tm = _token_tile(seq_len)
    pn, pgate, pproj = ple
    return pl.pallas_call(
        _post_attn_kernel,
        grid=(t // tm,),
        in_specs=[_rows(tm, D_MODEL), _rows(tm, Q_DIM), _resident(wo), *[_resident(w) for w in ffn],
                  _resident(pn), _resident(pgate), _layer_rows(layer, tm, PLE_DIM), _resident(pproj)],
        out_specs=_rows(tm, D_MODEL),
        out_shape=jax.ShapeDtypeStruct((t, D_MODEL), F32),
        compiler_params=_params(),
        name="post_attn",
    )(h, o, wo, *ffn, pn, pgate, p, pproj)


def _ffn_gmlp(h, ffn, gmlp, seq_len):
    t = h.shape[0]
    tm = _token_tile(seq_len)
    weights = (*ffn, *gmlp)
    return pl.pallas_call(
        _ffn_gmlp_kernel,
        grid=(t // tm,),
        in_specs=[_rows(tm, D_MODEL), *[_resident(w) for w in weights]],
        out_specs=_rows(tm, D_MODEL),
        out_shape=jax.ShapeDtypeStruct((t, D_MODEL), F32),
        compiler_params=_params(),
        name="ffn_gmlp",
    )(h, *weights)


def _ffn_ple(h, ffn, ple, p, layer, seq_len):
    t = h.shape[0]
    tm = _token_tile(seq_len)
    pn, pgate, pproj = ple
    return pl.pallas_call(
        _ffn_ple_kernel,
        grid=(t // tm,),
        in_specs=[_rows(tm, D_MODEL), *[_resident(w) for w in ffn], _resident(pn), _resident(pgate),
                  _layer_rows(layer, tm, PLE_DIM), _resident(pproj)],
        out_specs=_rows(tm, D_MODEL),
        out_shape=jax.ShapeDtypeStruct((t, D_MODEL), F32),
        compiler_params=_params(),
        name="ffn_ple",
    )(h, *ffn, pn, pgate, p, pproj)


def _rope_tables(seq_len):
    t = jnp.arange(seq_len, dtype=jnp.int32)
    row = (t // GRID_W).astype(F32)
    col = (t % GRID_W).astype(F32)
    inv = ROPE_THETA ** (-jnp.arange(0, ROPE_HALF, 2, dtype=F32) / ROPE_HALF)
    ang_r = row[:, None] * inv
    ang_c = col[:, None] * inv
    cos = jnp.concatenate([jnp.cos(ang_r)] * 2 + [jnp.cos(ang_c)] * 2, axis=-1)
    sin = jnp.concatenate([-jnp.sin(ang_r), jnp.sin(ang_r), -jnp.sin(ang_c), jnp.sin(ang_c)], axis=-1)
    return cos, sin


def _score_bound(q_gain, k_gain):
    bound = jnp.max(jnp.abs(q_gain)) * jnp.max(jnp.abs(k_gain)) * (HEAD_DIM ** 0.5 * LOG2_E * (1.0 + 2.0 ** -7))
    return bound.astype(F32).reshape(1, 1)


def _row(vec):
    return vec.reshape(1, -1).astype(F32)


def _trunk(x, p, params):
    batch, seq_len, _ = x.shape
    t = batch * seq_len
    h = x.reshape(t, D_MODEL)
    p = p.reshape(p.shape[0], t, PLE_DIM)
    cos, sin = _rope_tables(seq_len)
    l0, l1 = params

    h, q, k, vt = _pre_attn(h, seq_len, l0["ffn1"], l0["mix_norm"], l0["w_qkv"], l0["q_norm"], l0["k_norm"], cos, sin)
    o = _attention(q, k, vt, l0["score_bound"], batch, seq_len)
    h = _post_attn(h, o, l0["w_o"], l0["ffn2"], l0["ple"], p, 0, seq_len)

    h = _ffn_gmlp(h, l1["ffn1"], l1["gmlp"], seq_len)
    h = _ffn_ple(h, l1["ffn2"], l1["ple"], p, 1, seq_len)
    return h.reshape(batch, seq_len, D_MODEL)


def kernel(x_prompt, x_sample, p_prompt, p_sample, ffn1_norm, ffn1_w_gate, ffn1_w_up, ffn1_w_down, mix_norm,
           attn_w_qkv, attn_q_norm, attn_k_norm, attn_w_o, gmlp_w_in, gmlp_v_norm, gmlp_w_s, gmlp_b_s, gmlp_w_out,
           ffn2_norm, ffn2_w_gate, ffn2_w_up, ffn2_w_down, ple_norm, ple_w_gate, ple_w_proj):
    def ffn(norm, wg, wu, wd, i):
        return (_row(norm[i]), wg[i].astype(BF16), wu[i].astype(BF16), wd[i].astype(BF16))

    def ple(i):
        return (_row(ple_norm[i]), ple_w_gate[i].astype(BF16), ple_w_proj[i].astype(BF16))

    bias = jnp.repeat(gmlp_b_s[0].T.astype(F32), GMLP_GROUP_W, axis=1)
    layer0 = dict(
        ffn1=ffn(ffn1_norm, ffn1_w_gate, ffn1_w_up, ffn1_w_down, 0),
        mix_norm=_row(mix_norm[0]), w_qkv=attn_w_qkv[0].astype(BF16),
        q_norm=_row(attn_q_norm[0]), k_norm=_row(attn_k_norm[0]), w_o=attn_w_o[0].astype(BF16),
        score_bound=_score_bound(attn_q_norm[0], attn_k_norm[0]),
        ffn2=ffn(ffn2_norm, ffn2_w_gate, ffn2_w_up, ffn2_w_down, 0), ple=ple(0))
    layer1 = dict(
        ffn1=ffn(ffn1_norm, ffn1_w_gate, ffn1_w_up, ffn1_w_down, 1),
        gmlp=(_row(mix_norm[1]), gmlp_w_in[0].astype(BF16), _row(gmlp_v_norm[0]), gmlp_w_s[0].astype(BF16), bias,
              gmlp_w_out[0].astype(BF16)),
        ffn2=ffn(ffn2_norm, ffn2_w_gate, ffn2_w_up, ffn2_w_down, 1), ple=ple(1))
    params = (layer0, layer1)
    return (_trunk(x_prompt, p_prompt, params), _trunk(x_sample, p_sample, params))
```

```python
import jax
import jax.numpy as jnp
from jax import lax
from jax.experimental import pallas as pl
from jax.experimental.pallas import tpu as pltpu

D_MODEL = 1024
N_Q_HEADS = 8
N_KV_HEADS = 2
HEAD_DIM = 128
GQA_GROUP = N_Q_HEADS // N_KV_HEADS
Q_DIM = N_Q_HEADS * HEAD_DIM
KV_DIM = N_KV_HEADS * HEAD_DIM
QKV_DIM = Q_DIM + 2 * KV_DIM
ROPE_HALF = HEAD_DIM // 2
ROPE_QUARTER = ROPE_HALF // 2
ROPE_THETA = 10000.0
GRID_W = 64
GMLP_WIDTH = D_MODEL
GMLP_GROUPS = 8
GMLP_GROUP_W = GMLP_WIDTH // GMLP_GROUPS
CHUNK = 128
D_FF = 2816
PLE_DIM = 256
EPS = 1e-6

BF16 = jnp.bfloat16
F32 = jnp.float32

VMEM_LIMIT_BYTES = 56 * 1024 * 1024
TOKEN_TILE = 512
ATTN_Q_TILE = 512
VT_ROWS = HEAD_DIM + 16
ROW_BLOCKS = 2
LOG2_E = 1.4426950408889634
MIN_SOFTMAX_SUM = 2.0 ** -60


def _dot(a, b):
    return jnp.dot(a, b, preferred_element_type=F32)


def _rms(x, g):
    return x * lax.rsqrt(jnp.mean(x * x, axis=-1, keepdims=True) + EPS) * g


def _gelu_exact(x):
    return 0.5 * x * (1.0 + lax.erf(x * (2.0 ** -0.5)))


def _ffn_stage(hs, norm_ref, wg_ref, wu_ref, wd_ref):
    xns = [_rms(h, norm_ref[...]).astype(BF16) for h in hs]
    gs = [_dot(xn, wg_ref[...]) for xn in xns]
    us = [_dot(xn, wu_ref[...]) for xn in xns]
    acts = [(g * jax.nn.sigmoid(g) * u).astype(BF16) for g, u in zip(gs, us)]
    return [h + 0.5 * _dot(a, wd_ref[...]) for h, a in zip(hs, acts)]


def _ple_stage(hs, norm_ref, wgate_ref, ps, wproj_ref):
    xns = [_rms(h, norm_ref[...]).astype(BF16) for h in hs]
    gates = [jax.nn.sigmoid(_dot(xn, wgate_ref[...])) for xn in xns]
    return [h + gate * _dot(p.astype(BF16), wproj_ref[...]) for h, gate, p in zip(hs, gates, ps)]


def _rope_partner(x):
    lane = lax.broadcasted_iota(jnp.int32, x.shape, 1)
    first = (lane % ROPE_HALF) < ROPE_QUARTER
    up = pltpu.roll(x, HEAD_DIM - ROPE_QUARTER, 1)
    down = pltpu.roll(x, ROPE_QUARTER, 1)
    return jnp.where(first, up, down)


def _qk_head(x, gain, cos, sin, scale):
    y = _rms(x, gain)
    y = y * cos + _rope_partner(y) * sin
    if scale is not None:
        y = y * scale
    return y.astype(BF16)


def _qkv_heads(qkv, rows, qnorm_ref, knorm_ref, cos_ref, sin_ref, q_ref, k_ref, vt_ref):
    cos = cos_ref[rows, :]
    sin = sin_ref[rows, :]
    for hd in range(N_Q_HEADS):
        sl = slice(hd * HEAD_DIM, (hd + 1) * HEAD_DIM)
        q_ref[rows, sl] = _qk_head(qkv[:, sl], qnorm_ref[...], cos, sin, HEAD_DIM ** -0.5 * LOG2_E)
    for hd in range(N_KV_HEADS):
        sl = slice(hd * HEAD_DIM, (hd + 1) * HEAD_DIM)
        src = slice(Q_DIM + hd * HEAD_DIM, Q_DIM + (hd + 1) * HEAD_DIM)
        k_ref[rows, sl] = _qk_head(qkv[:, src], knorm_ref[...], cos, sin, None)
    ones = jnp.ones((VT_ROWS - HEAD_DIM, qkv.shape[0]), BF16)
    for hd in range(N_KV_HEADS):
        src = slice(Q_DIM + KV_DIM + hd * HEAD_DIM, Q_DIM + KV_DIM + (hd + 1) * HEAD_DIM)
        vt_ref[0, 0, hd * VT_ROWS:hd * VT_ROWS + HEAD_DIM, rows] = qkv[:, src].T.astype(BF16)
        vt_ref[0, 0, hd * VT_ROWS + HEAD_DIM:(hd + 1) * VT_ROWS, rows] = ones


def _spatial_gate(v, ws_ref, bs_ref):
    n_chunks = v.shape[0] // CHUNK
    cols = []
    for g in range(GMLP_GROUPS):
        gs = slice(g * GMLP_GROUP_W, (g + 1) * GMLP_GROUP_W)
        rhs = jnp.concatenate([v[c * CHUNK:(c + 1) * CHUNK, gs] for c in range(n_chunks)], axis=1)
        cols.append(_dot(ws_ref[g], rhs))
    rows = []
    for c in range(n_chunks):
        cs = slice(c * GMLP_GROUP_W, (c + 1) * GMLP_GROUP_W)
        rows.append(jnp.concatenate([cols[g][:, cs] for g in range(GMLP_GROUPS)], axis=1) + bs_ref[...])
    return jnp.concatenate(rows, axis=0)


def _gmlp_stage(hs, mixnorm_ref, win_ref, vnorm_ref, ws_ref, bs_ref, wout_ref):
    xns = [_rms(h, mixnorm_ref[...]).astype(BF16) for h in hs]
    zs = [_gelu_exact(_dot(xn, win_ref[...])) for xn in xns]
    vs = [_rms(z[:, GMLP_WIDTH:], vnorm_ref[...]).astype(BF16) for z in zs]
    svs = [_spatial_gate(v, ws_ref, bs_ref) for v in vs]
    gated = [(z[:, :GMLP_WIDTH] * sv).astype(BF16) for z, sv in zip(zs, svs)]
    return [h + _dot(x, wout_ref[...]) for h, x in zip(hs, gated)]


def _row_blocks(ref, n_blocks=ROW_BLOCKS):
    n = ref.shape[0] // n_blocks
    return [slice(i * n, (i + 1) * n) for i in range(n_blocks)]


def _pre_attn_kernel(h_ref, n1_ref, wg_ref, wu_ref, wd_ref, mixnorm_ref, wqkv_ref, qnorm_ref, knorm_ref,
                     cos_ref, sin_ref, hout_ref, q_ref, k_ref, vt_ref, qkv_ref):
    @pl.when(pl.program_id(0) == 0)
    def _():
        qkv_ref[...] = jnp.zeros(qkv_ref.shape, qkv_ref.dtype)

    blocks = _row_blocks(h_ref)
    hs = _ffn_stage([h_ref[rows, :] for rows in blocks], n1_ref, wg_ref, wu_ref, wd_ref)
    for rows in blocks:
        _qkv_heads(qkv_ref[rows, :], rows, qnorm_ref, knorm_ref, cos_ref, sin_ref, q_ref, k_ref, vt_ref)
    xns = [_rms(h, mixnorm_ref[...]).astype(BF16) for h in hs]
    for rows, h, xn in zip(blocks, hs, xns):
        hout_ref[rows, :] = h
        qkv_ref[rows, :] = _dot(xn, wqkv_ref[...])


def _nt_dot(a, b):
    return lax.dot_general(a, b, (((1,), (1,)), ((), ())), preferred_element_type=F32)


def _kv_chunk(k_ref, vt_ref, hk, j):
    tk = vt_ref.shape[3]
    return (k_ref[0, j * tk:(j + 1) * tk, hk * HEAD_DIM:(hk + 1) * HEAD_DIM],
            vt_ref[0, j, hk * VT_ROWS:(hk + 1) * VT_ROWS, :])


def _softmax_pv_bounded(q, k_ref, vt_ref, hk, shift):
    acc = None
    for j in range(vt_ref.shape[1]):
        k, vt = _kv_chunk(k_ref, vt_ref, hk, j)
        pv = _dot(vt, jnp.exp2(_nt_dot(k, q) - shift).astype(BF16))
        acc = pv if acc is None else acc + pv
    return acc


def _softmax_pv_online(q, k_ref, vt_ref, hk):
    m = acc = None
    for j in range(vt_ref.shape[1]):
        k, vt = _kv_chunk(k_ref, vt_ref, hk, j)
        s = _nt_dot(q, k)
        m_chunk = jnp.max(s, axis=-1, keepdims=True)
        m_new = m_chunk if m is None else jnp.maximum(m, m_chunk)
        pv = _nt_dot(jnp.exp2(s - m_new).astype(BF16), vt)
        acc = pv if m is None else jnp.exp2(m - m_new) * acc + pv
        m = m_new
    return acc


def _attn_kernel(shift_ref, q_ref, k_ref, vt_ref, o_ref):
    tq = q_ref.shape[1]

    def queries(hk):
        return jnp.concatenate([q_ref[0, :, (hk * GQA_GROUP + g) * HEAD_DIM:(hk * GQA_GROUP + g + 1) * HEAD_DIM]
                                for g in range(GQA_GROUP)], axis=0)

    def store(hk, o):
        for g in range(GQA_GROUP):
            hd = hk * GQA_GROUP + g
            o_ref[0, :, hd * HEAD_DIM:(hd + 1) * HEAD_DIM] = o[g * tq:(g + 1) * tq].astype(o_ref.dtype)

    accs = [_softmax_pv_bounded(queries(hk), k_ref, vt_ref, hk, shift_ref[0, 0]) for hk in range(N_KV_HEADS)]
    sums = [acc[HEAD_DIM:HEAD_DIM + 1] for acc in accs]
    for hk in range(N_KV_HEADS):
        store(hk, (accs[hk][:HEAD_DIM] / sums[hk]).T)

    @pl.when(jnp.logical_not(jnp.min(jnp.minimum(*sums)) >= MIN_SOFTMAX_SUM))
    def _():
        for hk in range(N_KV_HEADS):
            acc = _softmax_pv_online(queries(hk), k_ref, vt_ref, hk)
            store(hk, acc[:, :HEAD_DIM] / acc[:, HEAD_DIM:HEAD_DIM + 1])


def _post_attn_kernel(h_ref, o_ref, wo_ref, n2_ref, wg_ref, wu_ref, wd_ref, pn_ref, pgate_ref, p_ref, pproj_ref,
                      hout_ref):
    blocks = _row_blocks(h_ref)
    hs = [h_ref[rows, :] + _dot(o_ref[rows, :], wo_ref[...]) for rows in blocks]
    hs = _ffn_stage(hs, n2_ref, wg_ref, wu_ref, wd_ref)
    hs = _ple_stage(hs, pn_ref, pgate_ref, [p_ref[rows, :] for rows in blocks], pproj_ref)
    for rows, h in zip(blocks, hs):
        hout_ref[rows, :] = h


def _ffn_gmlp_kernel(h_ref, n1_ref, wg_ref, wu_ref, wd_ref, mixnorm_ref, win_ref, vnorm_ref, ws_ref, bs_ref,
                     wout_ref, hout_ref):
    blocks = _row_blocks(h_ref)
    hs = _ffn_stage([h_ref[rows, :] for rows in blocks], n1_ref, wg_ref, wu_ref, wd_ref)
    hs = _gmlp_stage(hs, mixnorm_ref, win_ref, vnorm_ref, ws_ref, bs_ref, wout_ref)
    for rows, h in zip(blocks, hs):
        hout_ref[rows, :] = h


def _ffn_ple_kernel(h_ref, n2_ref, wg_ref, wu_ref, wd_ref, pn_ref, pgate_ref, p_ref, pproj_ref, hout_ref):
    blocks = _row_blocks(h_ref)
    hs = _ffn_stage([h_ref[rows, :] for rows in blocks], n2_ref, wg_ref, wu_ref, wd_ref)
    hs = _ple_stage(hs, pn_ref, pgate_ref, [p_ref[rows, :] for rows in blocks], pproj_ref)
    for rows, h in zip(blocks, hs):
        hout_ref[rows, :] = h


def _resident(arr):
    nd = arr.ndim
    return pl.BlockSpec(arr.shape, lambda *_: (0,) * nd, pipeline_mode=pl.Buffered(1))


def _rows(tm, width):
    return pl.BlockSpec((tm, width), lambda i: (i, 0))


def _layer_rows(layer, tm, width):
    return pl.BlockSpec((None, tm, width), lambda i: (layer, i, 0))


def _params(n_axes=1):
    return pltpu.CompilerParams(dimension_semantics=("arbitrary",) * n_axes,
                                vmem_limit_bytes=VMEM_LIMIT_BYTES)


def _token_tile(seq_len):
    tm = min(TOKEN_TILE, seq_len)
    assert seq_len % tm == 0 and tm % (CHUNK * ROW_BLOCKS) == 0
    return tm


def _pre_attn(h, seq_len, ffn, mixnorm, wqkv, qnorm, knorm, cos, sin):
    t = h.shape[0]
    tm = _token_tile(seq_len)
    tiles_per_seq = seq_len // tm
    n_tiles = t // tm
    weights = (*ffn, mixnorm, wqkv, qnorm, knorm)

    def cur(i):
        return jnp.minimum(i, n_tiles - 1)

    def prev(i):
        return jnp.maximum(i - 1, 0)

    table = pl.BlockSpec((tm, HEAD_DIM), lambda i: (prev(i) % tiles_per_seq, 0))
    vt_spec = pl.BlockSpec((1, 1, N_KV_HEADS * VT_ROWS, tm),
                           lambda i: (prev(i) // tiles_per_seq, prev(i) % tiles_per_seq, 0, 0))
    return pl.pallas_call(
        _pre_attn_kernel,
        grid=(n_tiles + 1,),
        in_specs=[pl.BlockSpec((tm, D_MODEL), lambda i: (cur(i), 0)), *[_resident(w) for w in weights], table, table],
        out_specs=[pl.BlockSpec((tm, D_MODEL), lambda i: (cur(i), 0)),
                   pl.BlockSpec((tm, Q_DIM), lambda i: (prev(i), 0)),
                   pl.BlockSpec((tm, KV_DIM), lambda i: (prev(i), 0)), vt_spec],
        out_shape=[jax.ShapeDtypeStruct((t, D_MODEL), F32), jax.ShapeDtypeStruct((t, Q_DIM), BF16),
                   jax.ShapeDtypeStruct((t, KV_DIM), BF16),
                   jax.ShapeDtypeStruct((t // seq_len, tiles_per_seq, N_KV_HEADS * VT_ROWS, tm), BF16)],
        scratch_shapes=[pltpu.VMEM((tm, QKV_DIM), F32)],
        compiler_params=_params(),
        name="pre_attn",
    )(h, *weights, cos, sin)


def _attention(q, k, vt, shift, batch, seq_len):
    tq = min(ATTN_Q_TILE, seq_len)
    assert seq_len % tq == 0
    q3 = q.reshape(batch, seq_len, Q_DIM)
    k3 = k.reshape(batch, seq_len, KV_DIM)
    qspec = pl.BlockSpec((1, tq, Q_DIM), lambda b, i: (b, i, 0))
    kspec = pl.BlockSpec((1, seq_len, KV_DIM), lambda b, i: (b, 0, 0))
    vspec = pl.BlockSpec((1, *vt.shape[1:]), lambda b, i: (b, 0, 0, 0))
    o = pl.pallas_call(
        _attn_kernel,
        grid=(batch, seq_len // tq),
        in_specs=[pl.BlockSpec(memory_space=pltpu.SMEM), qspec, kspec, vspec],
        out_specs=qspec,
        out_shape=jax.ShapeDtypeStruct((batch, seq_len, Q_DIM), BF16),
        compiler_params=_params(2),
        name="attention",
    )(shift, q3, k3, vt)
    return o.reshape(batch * seq_len, Q_DIM)


def _post_attn(h, o, wo, ffn, ple, p, layer, seq_len):
    t = h.shape[0]
    tm = _token_tile(seq_len)
    pn, pgate, pproj = ple
    return pl.pallas_call(
        _post_attn_kernel,
        grid=(t // tm,),
        in_specs=[_rows(tm, D_MODEL), _rows(tm, Q_DIM), _resident(wo), *[_resident(w) for w in ffn],
                  _resident(pn), _resident(pgate), _layer_rows(layer, tm, PLE_DIM), _resident(pproj)],
        out_specs=_rows(tm, D_MODEL),
        out_shape=jax.ShapeDtypeStruct((t, D_MODEL), F32),
        compiler_params=_params(),
        name="post_attn",
    )(h, o, wo, *ffn, pn, pgate, p, pproj)


def _ffn_gmlp(h, ffn, gmlp, seq_len):
    t = h.shape[0]
    tm = _token_tile(seq_len)
    weights = (*ffn, *gmlp)
    return pl.pallas_call(
        _ffn_gmlp_kernel,
        grid=(t // tm,),
        in_specs=[_rows(tm, D_MODEL), *[_resident(w) for w in weights]],
        out_specs=_rows(tm, D_MODEL),
        out_shape=jax.ShapeDtypeStruct((t, D_MODEL), F32),
        compiler_params=_params(),
        name="ffn_gmlp",
    )(h, *weights)


def _ffn_ple(h, ffn, ple, p, layer, seq_len):
    t = h.shape[0]
    tm = _token_tile(seq_len)
    pn, pgate, pproj = ple
    return pl.pallas_call(
        _ffn_ple_kernel,
        grid=(t // tm,),
        in_specs=[_rows(tm, D_MODEL), *[_resident(w) for w in ffn], _resident(pn), _resident(pgate),
                  _layer_rows(layer, tm, PLE_DIM), _resident(pproj)],
        out_specs=_rows(tm, D_MODEL),
        out_shape=jax.ShapeDtypeStruct((t, D_MODEL), F32),
        compiler_params=_params(),
        name="ffn_ple",
    )(h, *ffn, pn, pgate, p, pproj)


def _rope_tables(seq_len):
    t = jnp.arange(seq_len, dtype=jnp.int32)
    row = (t // GRID_W).astype(F32)
    col = (t % GRID_W).astype(F32)
    inv = ROPE_THETA ** (-jnp.arange(0, ROPE_HALF, 2, dtype=F32) / ROPE_HALF)
    ang_r = row[:, None] * inv
    ang_c = col[:, None] * inv
    cos = jnp.concatenate([jnp.cos(ang_r)] * 2 + [jnp.cos(ang_c)] * 2, axis=-1)
    sin = jnp.concatenate([-jnp.sin(ang_r), jnp.sin(ang_r), -jnp.sin(ang_c), jnp.sin(ang_c)], axis=-1)
    return cos, sin


def _score_bound(q_gain, k_gain):
    bound = jnp.max(jnp.abs(q_gain)) * jnp.max(jnp.abs(k_gain)) * (HEAD_DIM ** 0.5 * LOG2_E * (1.0 + 2.0 ** -7))
    return bound.astype(F32).reshape(1, 1)


def _row(vec):
    return vec.reshape(1, -1).astype(F32)


def _trunk(x, p, params):
    batch, seq_len, _ = x.shape
    t = batch * seq_len
    h = x.reshape(t, D_MODEL)
    p = p.reshape(p.shape[0], t, PLE_DIM)
    cos, sin = _rope_tables(seq_len)
    l0, l1 = params

    h, q, k, vt = _pre_attn(h, seq_len, l0["ffn1"], l0["mix_norm"], l0["w_qkv"], l0["q_norm"], l0["k_norm"], cos, sin)
    o = _attention(q, k, vt, l0["score_bound"], batch, seq_len)
    h = _post_attn(h, o, l0["w_o"], l0["ffn2"], l0["ple"], p, 0, seq_len)

    h = _ffn_gmlp(h, l1["ffn1"], l1["gmlp"], seq_len)
    h = _ffn_ple(h, l1["ffn2"], l1["ple"], p, 1, seq_len)
    return h.reshape(batch, seq_len, D_MODEL)


def kernel(x_prompt, x_sample, p_prompt, p_sample, ffn1_norm, ffn1_w_gate, ffn1_w_up, ffn1_w_down, mix_norm,
           attn_w_qkv, attn_q_norm, attn_k_norm, attn_w_o, gmlp_w_in, gmlp_v_norm, gmlp_w_s, gmlp_b_s, gmlp_w_out,
           ffn2_norm, ffn2_w_gate, ffn2_w_up, ffn2_w_down, ple_norm, ple_w_gate, ple_w_proj):
    def ffn(norm, wg, wu, wd, i):
        return (_row(norm[i]), wg[i].astype(BF16), wu[i].astype(BF16), wd[i].astype(BF16))

    def ple(i):
        return (_row(ple_norm[i]), ple_w_gate[i].astype(BF16), ple_w_proj[i].astype(BF16))

    bias = jnp.repeat(gmlp_b_s[0].T.astype(F32), GMLP_GROUP_W, axis=1)
    layer0 = dict(
        ffn1=ffn(ffn1_norm, ffn1_w_gate, ffn1_w_up, ffn1_w_down, 0),
        mix_norm=_row(mix_norm[0]), w_qkv=attn_w_qkv[0].astype(BF16),
        q_norm=_row(attn_q_norm[0]), k_norm=_row(attn_k_norm[0]), w_o=attn_w_o[0].astype(BF16),
        score_bound=_score_bound(attn_q_norm[0], attn_k_norm[0]),
        ffn2=ffn(ffn2_norm, ffn2_w_gate, ffn2_w_up, ffn2_w_down, 0), ple=ple(0))
    layer1 = dict(
        ffn1=ffn(ffn1_norm, ffn1_w_gate, ffn1_w_up, ffn1_w_down, 1),
        gmlp=(_row(mix_norm[1]), gmlp_w_in[0].astype(BF16), _row(gmlp_v_norm[0]), gmlp_w_s[0].astype(BF16), bias,
              gmlp_w_out[0].astype(BF16)),
        ffn2=ffn(ffn2_norm, ffn2_w_gate, ffn2_w_up, ffn2_w_down, 1), ple=ple(1))
    params = (layer0, layer1)
    return (_trunk(x_prompt, p_prompt, params), _trunk(x_sample, p_sample, params))
```

```python
import jax
import jax.numpy as jnp
from jax import lax
from jax.experimental import pallas as pl
from jax.experimental.pallas import tpu as pltpu

D_MODEL = 1024
N_Q_HEADS = 8
N_KV_HEADS = 2
HEAD_DIM = 128
GQA_GROUP = N_Q_HEADS // N_KV_HEADS
Q_DIM = N_Q_HEADS * HEAD_DIM
KV_DIM = N_KV_HEADS * HEAD_DIM
QKV_DIM = Q_DIM + 2 * KV_DIM
ROPE_HALF = HEAD_DIM // 2
ROPE_QUARTER = ROPE_HALF // 2
ROPE_THETA = 10000.0
GRID_W = 64
GMLP_WIDTH = D_MODEL
GMLP_GROUPS = 8
GMLP_GROUP_W = GMLP_WIDTH // GMLP_GROUPS
CHUNK = 128
D_FF = 2816
PLE_DIM = 256
EPS = 1e-6

BF16 = jnp.bfloat16
F32 = jnp.float32

VMEM_LIMIT_BYTES = 56 * 1024 * 1024
TOKEN_TILE = 512
ATTN_Q_TILE = 512
VT_ROWS = HEAD_DIM + 16
ROW_BLOCK = 256
FFN_PLE_TOKEN_TILE = 1024
LOG2_E = 1.4426950408889634
MIN_SOFTMAX_SUM = 2.0 ** -60


def _dot(a, b):
    return jnp.dot(a, b, preferred_element_type=F32)


def _rms(x, g):
    return x * lax.rsqrt(jnp.mean(x * x, axis=-1, keepdims=True) + EPS) * g


def _gelu_exact(x):
    return 0.5 * x * (1.0 + lax.erf(x * (2.0 ** -0.5)))


def _ffn_stage(hs, norm_ref, wg_ref, wu_ref, wd_ref):
    xns = [_rms(h, norm_ref[...]).astype(BF16) for h in hs]
    gs = [_dot(xn, wg_ref[...]) for xn in xns]
    us = [_dot(xn, wu_ref[...]) for xn in xns]
    acts = [(g * jax.nn.sigmoid(g) * u).astype(BF16) for g, u in zip(gs, us)]
    return [h + 0.5 * _dot(a, wd_ref[...]) for h, a in zip(hs, acts)]


def _ple_stage(hs, norm_ref, wgate_ref, ps, wproj_ref):
    xns = [_rms(h, norm_ref[...]).astype(BF16) for h in hs]
    gates = [jax.nn.sigmoid(_dot(xn, wgate_ref[...])) for xn in xns]
    return [h + gate * _dot(p.astype(BF16), wproj_ref[...]) for h, gate, p in zip(hs, gates, ps)]


def _rope_partner(x):
    lane = lax.broadcasted_iota(jnp.int32, x.shape, 1)
    first = (lane % ROPE_HALF) < ROPE_QUARTER
    up = pltpu.roll(x, HEAD_DIM - ROPE_QUARTER, 1)
    down = pltpu.roll(x, ROPE_QUARTER, 1)
    return jnp.where(first, up, down)


def _qk_head(x, gain, cos, sin, scale):
    y = _rms(x, gain)
    y = y * cos + _rope_partner(y) * sin
    if scale is not None:
        y = y * scale
    return y.astype(BF16)


def _qkv_heads(qkv, rows, qnorm_ref, knorm_ref, cos_ref, sin_ref, q_ref, k_ref, vt_ref):
    cos = cos_ref[rows, :]
    sin = sin_ref[rows, :]
    for hd in range(N_Q_HEADS):
        sl = slice(hd * HEAD_DIM, (hd + 1) * HEAD_DIM)
        q_ref[rows, sl] = _qk_head(qkv[:, sl], qnorm_ref[...], cos, sin, HEAD_DIM ** -0.5 * LOG2_E)
    for hd in range(N_KV_HEADS):
        sl = slice(hd * HEAD_DIM, (hd + 1) * HEAD_DIM)
        src = slice(Q_DIM + hd * HEAD_DIM, Q_DIM + (hd + 1) * HEAD_DIM)
        k_ref[rows, sl] = _qk_head(qkv[:, src], knorm_ref[...], cos, sin, None)
    ones = jnp.ones((VT_ROWS - HEAD_DIM, qkv.shape[0]), BF16)
    for hd in range(N_KV_HEADS):
        src = slice(Q_DIM + KV_DIM + hd * HEAD_DIM, Q_DIM + KV_DIM + (hd + 1) * HEAD_DIM)
        vt_ref[0, 0, hd * VT_ROWS:hd * VT_ROWS + HEAD_DIM, rows] = qkv[:, src].T.astype(BF16)
        vt_ref[0, 0, hd * VT_ROWS + HEAD_DIM:(hd + 1) * VT_ROWS, rows] = ones


def _spatial_gate(v, ws_ref, bs_ref):
    n_chunks = v.shape[0] // CHUNK
    cols = []
    for g in range(GMLP_GROUPS):
        gs = slice(g * GMLP_GROUP_W, (g + 1) * GMLP_GROUP_W)
        rhs = jnp.concatenate([v[c * CHUNK:(c + 1) * CHUNK, gs] for c in range(n_chunks)], axis=1)
        cols.append(_dot(ws_ref[g], rhs))
    rows = []
    for c in range(n_chunks):
        cs = slice(c * GMLP_GROUP_W, (c + 1) * GMLP_GROUP_W)
        rows.append(jnp.concatenate([cols[g][:, cs] for g in range(GMLP_GROUPS)], axis=1) + bs_ref[...])
    return jnp.concatenate(rows, axis=0)


def _gmlp_stage(hs, mixnorm_ref, win_ref, vnorm_ref, ws_ref, bs_ref, wout_ref):
    xns = [_rms(h, mixnorm_ref[...]).astype(BF16) for h in hs]
    zs = [_gelu_exact(_dot(xn, win_ref[...])) for xn in xns]
    vs = [_rms(z[:, GMLP_WIDTH:], vnorm_ref[...]).astype(BF16) for z in zs]
    svs = [_spatial_gate(v, ws_ref, bs_ref) for v in vs]
    gated = [(z[:, :GMLP_WIDTH] * sv).astype(BF16) for z, sv in zip(zs, svs)]
    return [h + _dot(x, wout_ref[...]) for h, x in zip(hs, gated)]


def _row_blocks(ref):
    return [slice(i, i + ROW_BLOCK) for i in range(0, ref.shape[0], ROW_BLOCK)]


def _pre_attn_kernel(h_ref, n1_ref, wg_ref, wu_ref, wd_ref, mixnorm_ref, wqkv_ref, qnorm_ref, knorm_ref,
                     cos_ref, sin_ref, hout_ref, q_ref, k_ref, vt_ref, qkv_ref):
    @pl.when(pl.program_id(0) == 0)
    def _():
        qkv_ref[...] = jnp.zeros(qkv_ref.shape, qkv_ref.dtype)

    blocks = _row_blocks(h_ref)
    hs = _ffn_stage([h_ref[rows, :] for rows in blocks], n1_ref, wg_ref, wu_ref, wd_ref)
    for rows in blocks:
        _qkv_heads(qkv_ref[rows, :], rows, qnorm_ref, knorm_ref, cos_ref, sin_ref, q_ref, k_ref, vt_ref)
    xns = [_rms(h, mixnorm_ref[...]).astype(BF16) for h in hs]
    for rows, h, xn in zip(blocks, hs, xns):
        hout_ref[rows, :] = h
        qkv_ref[rows, :] = _dot(xn, wqkv_ref[...])


def _nt_dot(a, b):
    return lax.dot_general(a, b, (((1,), (1,)), ((), ())), preferred_element_type=F32)


def _kv_chunk(k_ref, vt_ref, hk, j):
    tk = vt_ref.shape[3]
    return (k_ref[0, j * tk:(j + 1) * tk, hk * HEAD_DIM:(hk + 1) * HEAD_DIM],
            vt_ref[0, j, hk * VT_ROWS:(hk + 1) * VT_ROWS, :])


def _softmax_pv_bounded(q, k_ref, vt_ref, hk, shift):
    acc = None
    for j in range(vt_ref.shape[1]):
        k, vt = _kv_chunk(k_ref, vt_ref, hk, j)
        pv = _dot(vt, jnp.exp2(_nt_dot(k, q) - shift).astype(BF16))
        acc = pv if acc is None else acc + pv
    return acc


def _softmax_pv_online(q, k_ref, vt_ref, hk):
    m = acc = None
    for j in range(vt_ref.shape[1]):
        k, vt = _kv_chunk(k_ref, vt_ref, hk, j)
        s = _nt_dot(q, k)
        m_chunk = jnp.max(s, axis=-1, keepdims=True)
        m_new = m_chunk if m is None else jnp.maximum(m, m_chunk)
        pv = _nt_dot(jnp.exp2(s - m_new).astype(BF16), vt)
        acc = pv if m is None else jnp.exp2(m - m_new) * acc + pv
        m = m_new
    return acc


def _attn_kernel(shift_ref, q_ref, k_ref, vt_ref, o_ref):
    tq = q_ref.shape[1]

    def queries(hk):
        return jnp.concatenate([q_ref[0, :, (hk * GQA_GROUP + g) * HEAD_DIM:(hk * GQA_GROUP + g + 1) * HEAD_DIM]
                                for g in range(GQA_GROUP)], axis=0)

    def store(hk, o):
        for g in range(GQA_GROUP):
            hd = hk * GQA_GROUP + g
            o_ref[0, :, hd * HEAD_DIM:(hd + 1) * HEAD_DIM] = o[g * tq:(g + 1) * tq].astype(o_ref.dtype)

    accs = [_softmax_pv_bounded(queries(hk), k_ref, vt_ref, hk, shift_ref[0, 0]) for hk in range(N_KV_HEADS)]
    sums = [acc[HEAD_DIM:HEAD_DIM + 1] for acc in accs]
    for hk in range(N_KV_HEADS):
        store(hk, (accs[hk][:HEAD_DIM] / sums[hk]).T)

    @pl.when(jnp.logical_not(jnp.min(jnp.minimum(*sums)) >= MIN_SOFTMAX_SUM))
    def _():
        for hk in range(N_KV_HEADS):
            acc = _softmax_pv_online(queries(hk), k_ref, vt_ref, hk)
            store(hk, acc[:, :HEAD_DIM] / acc[:, HEAD_DIM:HEAD_DIM + 1])


def _post_attn_kernel(h_ref, o_ref, wo_ref, n2_ref, wg_ref, wu_ref, wd_ref, pn_ref, pgate_ref, p_ref, pproj_ref,
                      hout_ref):
    blocks = _row_blocks(h_ref)
    hs = [h_ref[rows, :] + _dot(o_ref[rows, :], wo_ref[...]) for rows in blocks]
    hs = _ffn_stage(hs, n2_ref, wg_ref, wu_ref, wd_ref)
    hs = _ple_stage(hs, pn_ref, pgate_ref, [p_ref[rows, :] for rows in blocks], pproj_ref)
    for rows, h in zip(blocks, hs):
        hout_ref[rows, :] = h


def _ffn_gmlp_kernel(h_ref, n1_ref, wg_ref, wu_ref, wd_ref, mixnorm_ref, win_ref, vnorm_ref, ws_ref, bs_ref,
                     wout_ref, hout_ref):
    blocks = _row_blocks(h_ref)
    hs = _ffn_stage([h_ref[rows, :] for rows in blocks], n1_ref, wg_ref, wu_ref, wd_ref)
    hs = _gmlp_stage(hs, mixnorm_ref, win_ref, vnorm_ref, ws_ref, bs_ref, wout_ref)
    for rows, h in zip(blocks, hs):
        hout_ref[rows, :] = h


def _ffn_ple_kernel(h_ref, n2_ref, wg_ref, wu_ref, wd_ref, pn_ref, pgate_ref, p_ref, pproj_ref, hout_ref):
    blocks = _row_blocks(h_ref)
    hs = _ffn_stage([h_ref[rows, :] for rows in blocks], n2_ref, wg_ref, wu_ref, wd_ref)
    hs = _ple_stage(hs, pn_ref, pgate_ref, [p_ref[rows, :] for rows in blocks], pproj_ref)
    for rows, h in zip(blocks, hs):
        hout_ref[rows, :] = h


def _resident(arr):
    nd = arr.ndim
    return pl.BlockSpec(arr.shape, lambda *_: (0,) * nd, pipeline_mode=pl.Buffered(1))


def _rows(tm, width):
    return pl.BlockSpec((tm, width), lambda i: (i, 0))


def _layer_rows(layer, tm, width):
    return pl.BlockSpec((None, tm, width), lambda i: (layer, i, 0))


def _params(n_axes=1):
    return pltpu.CompilerParams(dimension_semantics=("arbitrary",) * n_axes,
                                vmem_limit_bytes=VMEM_LIMIT_BYTES)


def _token_tile(seq_len, tile=TOKEN_TILE):
    tm = min(tile, seq_len)
    assert seq_len % tm == 0 and tm % ROW_BLOCK == 0 and ROW_BLOCK % CHUNK == 0
    return tm


def _pre_attn(h, seq_len, ffn, mixnorm, wqkv, qnorm, knorm, cos, sin):
    t = h.shape[0]
    tm = _token_tile(seq_len)
    tiles_per_seq = seq_len // tm
    n_tiles = t // tm
    weights = (*ffn, mixnorm, wqkv, qnorm, knorm)

    def cur(i):
        return jnp.minimum(i, n_tiles - 1)

    def prev(i):
        return jnp.maximum(i - 1, 0)

    table = pl.BlockSpec((tm, HEAD_DIM), lambda i: (prev(i) % tiles_per_seq, 0))
    vt_spec = pl.BlockSpec((1, 1, N_KV_HEADS * VT_ROWS, tm),
                           lambda i: (prev(i) // tiles_per_seq, prev(i) % tiles_per_seq, 0, 0))
    return pl.pallas_call(
        _pre_attn_kernel,
        grid=(n_tiles + 1,),
        in_specs=[pl.BlockSpec((tm, D_MODEL), lambda i: (cur(i), 0)), *[_resident(w) for w in weights], table, table],
        out_specs=[pl.BlockSpec((tm, D_MODEL), lambda i: (cur(i), 0)),
                   pl.BlockSpec((tm, Q_DIM), lambda i: (prev(i), 0)),
                   pl.BlockSpec((tm, KV_DIM), lambda i: (prev(i), 0)), vt_spec],
        out_shape=[jax.ShapeDtypeStruct((t, D_MODEL), F32), jax.ShapeDtypeStruct((t, Q_DIM), BF16),
                   jax.ShapeDtypeStruct((t, KV_DIM), BF16),
                   jax.ShapeDtypeStruct((t // seq_len, tiles_per_seq, N_KV_HEADS * VT_ROWS, tm), BF16)],
        scratch_shapes=[pltpu.VMEM((tm, QKV_DIM), F32)],
        compiler_params=_params(),
        name="pre_attn",
    )(h, *weights, cos, sin)


def _attention(q, k, vt, shift, batch, seq_len):
    tq = min(ATTN_Q_TILE, seq_len)
    assert seq_len % tq == 0
    q3 = q.reshape(batch, seq_len, Q_DIM)
    k3 = k.reshape(batch, seq_len, KV_DIM)
    qspec = pl.BlockSpec((1, tq, Q_DIM), lambda b, i: (b, i, 0))
    kspec = pl.BlockSpec((1, seq_len, KV_DIM), lambda b, i: (b, 0, 0))
    vspec = pl.BlockSpec((1, *vt.shape[1:]), lambda b, i: (b, 0, 0, 0))
    o = pl.pallas_call(
        _attn_kernel,
        grid=(batch, seq_len // tq),
        in_specs=[pl.BlockSpec(memory_space=pltpu.SMEM), qspec, kspec, vspec],
        out_specs=qspec,
        out_shape=jax.ShapeDtypeStruct((batch, seq_len, Q_DIM), BF16),
        compiler_params=_params(2),
        name="attention",
    )(shift, q3, k3, vt)
    return o.reshape(batch * seq_len, Q_DIM)


def _post_attn(h, o, wo, ffn, ple, p, layer, seq_len):
    t = h.shape[0]
    tm = _token_tile(seq_len)
    pn, pgate, pproj = ple
    return pl.pallas_call(
        _post_attn_kernel,
        grid=(t // tm,),
        in_specs=[_rows(tm, D_MODEL), _rows(tm, Q_DIM), _resident(wo), *[_resident(w) for w in ffn],
                  _resident(pn), _resident(pgate), _layer_rows(layer, tm, PLE_DIM), _resident(pproj)],
        out_specs=_rows(tm, D_MODEL),
        out_shape=jax.ShapeDtypeStruct((t, D_MODEL), F32),
        compiler_params=_params(),
        name="post_attn",
    )(h, o, wo, *ffn, pn, pgate, p, pproj)


def _ffn_gmlp(h, ffn, gmlp, seq_len):
    t = h.shape[0]
    tm = _token_tile(seq_len)
    weights = (*ffn, *gmlp)
    return pl.pallas_call(
        _ffn_gmlp_kernel,
        grid=(t // tm,),
        in_specs=[_rows(tm, D_MODEL), *[_resident(w) for w in weights]],
        out_specs=_rows(tm, D_MODEL),
        out_shape=jax.ShapeDtypeStruct((t, D_MODEL), F32),
        compiler_params=_params(),
        name="ffn_gmlp",
    )(h, *weights)


def _ffn_ple(h, ffn, ple, p, layer, seq_len):
    t = h.shape[0]
    tm = _token_tile(seq_len, FFN_PLE_TOKEN_TILE)
    pn, pgate, pproj = ple
    return pl.pallas_call(
        _ffn_ple_kernel,
        grid=(t // tm,),
        in_specs=[_rows(tm, D_MODEL), *[_resident(w) for w in ffn], _resident(pn), _resident(pgate),
                  _layer_rows(layer, tm, PLE_DIM), _resident(pproj)],
        out_specs=_rows(tm, D_MODEL),
        out_shape=jax.ShapeDtypeStruct((t, D_MODEL), F32),
        compiler_params=_params(),
        name="ffn_ple",
    )(h, *ffn, pn, pgate, p, pproj)


def _rope_tables(seq_len):
    t = jnp.arange(seq_len, dtype=jnp.int32)
    row = (t // GRID_W).astype(F32)
    col = (t % GRID_W).astype(F32)
    inv = ROPE_THETA ** (-jnp.arange(0, ROPE_HALF, 2, dtype=F32) / ROPE_HALF)
    ang_r = row[:, None] * inv
    ang_c = col[:, None] * inv
    cos = jnp.concatenate([jnp.cos(ang_r)] * 2 + [jnp.cos(ang_c)] * 2, axis=-1)
    sin = jnp.concatenate([-jnp.sin(ang_r), jnp.sin(ang_r), -jnp.sin(ang_c), jnp.sin(ang_c)], axis=-1)
    return cos, sin


def _score_bound(q_gain, k_gain):
    bound = jnp.max(jnp.abs(q_gain)) * jnp.max(jnp.abs(k_gain)) * (HEAD_DIM ** 0.5 * LOG2_E * (1.0 + 2.0 ** -7))
    return bound.astype(F32).reshape(1, 1)


def _row(vec):
    return vec.reshape(1, -1).astype(F32)


def _trunk(x, p, params):
    batch, seq_len, _ = x.shape
    t = batch * seq_len
    h = x.reshape(t, D_MODEL)
    p = p.reshape(p.shape[0], t, PLE_DIM)
    cos, sin = _rope_tables(seq_len)
    l0, l1 = params

    h, q, k, vt = _pre_attn(h, seq_len, l0["ffn1"], l0["mix_norm"], l0["w_qkv"], l0["q_norm"], l0["k_norm"], cos, sin)
    o = _attention(q, k, vt, l0["score_bound"], batch, seq_len)
    h = _post_attn(h, o, l0["w_o"], l0["ffn2"], l0["ple"], p, 0, seq_len)

    h = _ffn_gmlp(h, l1["ffn1"], l1["gmlp"], seq_len)
    h = _ffn_ple(h, l1["ffn2"], l1["ple"], p, 1, seq_len)
    return h.reshape(batch, seq_len, D_MODEL)


def kernel(x_prompt, x_sample, p_prompt, p_sample, ffn1_norm, ffn1_w_gate, ffn1_w_up, ffn1_w_down, mix_norm,
           attn_w_qkv, attn_q_norm, attn_k_norm, attn_w_o, gmlp_w_in, gmlp_v_norm, gmlp_w_s, gmlp_b_s, gmlp_w_out,
           ffn2_norm, ffn2_w_gate, ffn2_w_up, ffn2_w_down, ple_norm, ple_w_gate, ple_w_proj):
    def ffn(norm, wg, wu, wd, i):
        return (_row(norm[i]), wg[i].astype(BF16), wu[i].astype(BF16), wd[i].astype(BF16))

    def ple(i):
        return (_row(ple_norm[i]), ple_w_gate[i].astype(BF16), ple_w_proj[i].astype(BF16))

    bias = jnp.repeat(gmlp_b_s[0].T.astype(F32), GMLP_GROUP_W, axis=1)
    layer0 = dict(
        ffn1=ffn(ffn1_norm, ffn1_w_gate, ffn1_w_up, ffn1_w_down, 0),
        mix_norm=_row(mix_norm[0]), w_qkv=attn_w_qkv[0].astype(BF16),
        q_norm=_row(attn_q_norm[0]), k_norm=_row(attn_k_norm[0]), w_o=attn_w_o[0].astype(BF16),
        score_bound=_score_bound(attn_q_norm[0], attn_k_norm[0]),
        ffn2=ffn(ffn2_norm, ffn2_w_gate, ffn2_w_up, ffn2_w_down, 0), ple=ple(0))
    layer1 = dict(
        ffn1=ffn(ffn1_norm, ffn1_w_gate, ffn1_w_up, ffn1_w_down, 1),
        gmlp=(_row(mix_norm[1]), gmlp_w_in[0].astype(BF16), _row(gmlp_v_norm[0]), gmlp_w_s[0].astype(BF16), bias,
              gmlp_w_out[0].astype(BF16)),
        ffn2=ffn(ffn2_norm, ffn2_w_gate, ffn2_w_up, ffn2_w_down, 1), ple=ple(1))
    params = (layer0, layer1)
    return (_trunk(x_prompt, p_prompt, params), _trunk(x_sample, p_sample, params))
```

```python
import jax
import jax.numpy as jnp
from jax import lax
from jax.experimental import pallas as pl
from jax.experimental.pallas import tpu as pltpu

D_MODEL = 1024
N_Q_HEADS = 8
N_KV_HEADS = 2
HEAD_DIM = 128
GQA_GROUP = N_Q_HEADS // N_KV_HEADS
Q_DIM = N_Q_HEADS * HEAD_DIM
KV_DIM = N_KV_HEADS * HEAD_DIM
QKV_DIM = Q_DIM + 2 * KV_DIM
ROPE_HALF = HEAD_DIM // 2
ROPE_QUARTER = ROPE_HALF // 2
ROPE_THETA = 10000.0
GRID_W = 64
GMLP_WIDTH = D_MODEL
GMLP_GROUPS = 8
GMLP_GROUP_W = GMLP_WIDTH // GMLP_GROUPS
CHUNK = 128
D_FF = 2816
PLE_DIM = 256
EPS = 1e-6

BF16 = jnp.bfloat16
F32 = jnp.float32

VMEM_LIMIT_BYTES = 56 * 1024 * 1024
TOKEN_TILE = 512
ATTN_Q_TILE = 512
VT_ROWS = HEAD_DIM + 16
ROW_BLOCK = 256
LAYER1_TOKEN_TILE = 1024
LOG2_E = 1.4426950408889634
MIN_SOFTMAX_SUM = 2.0 ** -60


def _dot(a, b):
    return jnp.dot(a, b, preferred_element_type=F32)


def _rms(x, g):
    return x * lax.rsqrt(jnp.mean(x * x, axis=-1, keepdims=True) + EPS) * g


def _gelu_exact(x):
    return 0.5 * x * (1.0 + lax.erf(x * (2.0 ** -0.5)))


def _ffn_stage(hs, norm_ref, wg_ref, wu_ref, wd_ref):
    xns = [_rms(h, norm_ref[...]).astype(BF16) for h in hs]
    gs = [_dot(xn, wg_ref[...]) for xn in xns]
    us = [_dot(xn, wu_ref[...]) for xn in xns]
    acts = [(g * jax.nn.sigmoid(g) * u).astype(BF16) for g, u in zip(gs, us)]
    return [h + 0.5 * _dot(a, wd_ref[...]) for h, a in zip(hs, acts)]


def _ple_stage(hs, norm_ref, wgate_ref, ps, wproj_ref):
    xns = [_rms(h, norm_ref[...]).astype(BF16) for h in hs]
    gates = [jax.nn.sigmoid(_dot(xn, wgate_ref[...])) for xn in xns]
    return [h + gate * _dot(p.astype(BF16), wproj_ref[...]) for h, gate, p in zip(hs, gates, ps)]


def _rope_partner(x):
    lane = lax.broadcasted_iota(jnp.int32, x.shape, 1)
    first = (lane % ROPE_HALF) < ROPE_QUARTER
    up = pltpu.roll(x, HEAD_DIM - ROPE_QUARTER, 1)
    down = pltpu.roll(x, ROPE_QUARTER, 1)
    return jnp.where(first, up, down)


def _qk_head(x, gain, cos, sin, scale):
    y = _rms(x, gain)
    y = y * cos + _rope_partner(y) * sin
    if scale is not None:
        y = y * scale
    return y.astype(BF16)


def _qkv_heads(qkv, rows, qnorm_ref, knorm_ref, cos_ref, sin_ref, q_ref, k_ref, vt_ref):
    cos = cos_ref[rows, :]
    sin = sin_ref[rows, :]
    for hd in range(N_Q_HEADS):
        sl = slice(hd * HEAD_DIM, (hd + 1) * HEAD_DIM)
        q_ref[rows, sl] = _qk_head(qkv[:, sl], qnorm_ref[...], cos, sin, HEAD_DIM ** -0.5 * LOG2_E)
    for hd in range(N_KV_HEADS):
        sl = slice(hd * HEAD_DIM, (hd + 1) * HEAD_DIM)
        src = slice(Q_DIM + hd * HEAD_DIM, Q_DIM + (hd + 1) * HEAD_DIM)
        k_ref[rows, sl] = _qk_head(qkv[:, src], knorm_ref[...], cos, sin, None)
    ones = jnp.ones((VT_ROWS - HEAD_DIM, qkv.shape[0]), BF16)
    for hd in range(N_KV_HEADS):
        src = slice(Q_DIM + KV_DIM + hd * HEAD_DIM, Q_DIM + KV_DIM + (hd + 1) * HEAD_DIM)
        vt_ref[0, 0, hd * VT_ROWS:hd * VT_ROWS + HEAD_DIM, rows] = qkv[:, src].T.astype(BF16)
        vt_ref[0, 0, hd * VT_ROWS + HEAD_DIM:(hd + 1) * VT_ROWS, rows] = ones


def _spatial_gate(v, ws_ref, bs_ref):
    n_chunks = v.shape[0] // CHUNK
    cols = []
    for g in range(GMLP_GROUPS):
        gs = slice(g * GMLP_GROUP_W, (g + 1) * GMLP_GROUP_W)
        rhs = jnp.concatenate([v[c * CHUNK:(c + 1) * CHUNK, gs] for c in range(n_chunks)], axis=1)
        cols.append(_dot(ws_ref[g], rhs))
    rows = []
    for c in range(n_chunks):
        cs = slice(c * GMLP_GROUP_W, (c + 1) * GMLP_GROUP_W)
        rows.append(jnp.concatenate([cols[g][:, cs] for g in range(GMLP_GROUPS)], axis=1) + bs_ref[...])
    return jnp.concatenate(rows, axis=0)


def _gmlp_stage(hs, mixnorm_ref, win_ref, vnorm_ref, ws_ref, bs_ref, wout_ref):
    xns = [_rms(h, mixnorm_ref[...]).astype(BF16) for h in hs]
    zs = [_gelu_exact(_dot(xn, win_ref[...])) for xn in xns]
    vs = [_rms(z[:, GMLP_WIDTH:], vnorm_ref[...]).astype(BF16) for z in zs]
    svs = [_spatial_gate(v, ws_ref, bs_ref) for v in vs]
    gated = [(z[:, :GMLP_WIDTH] * sv).astype(BF16) for z, sv in zip(zs, svs)]
    return [h + _dot(x, wout_ref[...]) for h, x in zip(hs, gated)]


def _row_blocks(ref):
    return [slice(i, i + ROW_BLOCK) for i in range(0, ref.shape[0], ROW_BLOCK)]


def _pre_attn_kernel(h_ref, n1_ref, wg_ref, wu_ref, wd_ref, mixnorm_ref, wqkv_ref, qnorm_ref, knorm_ref,
                     cos_ref, sin_ref, hout_ref, q_ref, k_ref, vt_ref, qkv_ref):
    @pl.when(pl.program_id(0) == 0)
    def _():
        qkv_ref[...] = jnp.zeros(qkv_ref.shape, qkv_ref.dtype)

    blocks = _row_blocks(h_ref)
    hs = _ffn_stage([h_ref[rows, :] for rows in blocks], n1_ref, wg_ref, wu_ref, wd_ref)
    for rows in blocks:
        _qkv_heads(qkv_ref[rows, :], rows, qnorm_ref, knorm_ref, cos_ref, sin_ref, q_ref, k_ref, vt_ref)
    xns = [_rms(h, mixnorm_ref[...]).astype(BF16) for h in hs]
    for rows, h, xn in zip(blocks, hs, xns):
        hout_ref[rows, :] = h
        qkv_ref[rows, :] = _dot(xn, wqkv_ref[...])


def _nt_dot(a, b):
    return lax.dot_general(a, b, (((1,), (1,)), ((), ())), preferred_element_type=F32)


def _kv_chunk(k_ref, vt_ref, hk, j):
    tk = vt_ref.shape[3]
    return (k_ref[0, j * tk:(j + 1) * tk, hk * HEAD_DIM:(hk + 1) * HEAD_DIM],
            vt_ref[0, j, hk * VT_ROWS:(hk + 1) * VT_ROWS, :])


def _softmax_pv_bounded(q, k_ref, vt_ref, hk, shift):
    acc = None
    for j in range(vt_ref.shape[1]):
        k, vt = _kv_chunk(k_ref, vt_ref, hk, j)
        pv = _dot(vt, jnp.exp2(_nt_dot(k, q) - shift).astype(BF16))
        acc = pv if acc is None else acc + pv
    return acc


def _softmax_pv_online(q, k_ref, vt_ref, hk):
    m = acc = None
    for j in range(vt_ref.shape[1]):
        k, vt = _kv_chunk(k_ref, vt_ref, hk, j)
        s = _nt_dot(q, k)
        m_chunk = jnp.max(s, axis=-1, keepdims=True)
        m_new = m_chunk if m is None else jnp.maximum(m, m_chunk)
        pv = _nt_dot(jnp.exp2(s - m_new).astype(BF16), vt)
        acc = pv if m is None else jnp.exp2(m - m_new) * acc + pv
        m = m_new
    return acc


def _attn_kernel(shift_ref, q_ref, k_ref, vt_ref, o_ref):
    tq = q_ref.shape[1]

    def queries(hk):
        return jnp.concatenate([q_ref[0, :, (hk * GQA_GROUP + g) * HEAD_DIM:(hk * GQA_GROUP + g + 1) * HEAD_DIM]
                                for g in range(GQA_GROUP)], axis=0)

    def store(hk, o):
        for g in range(GQA_GROUP):
            hd = hk * GQA_GROUP + g
            o_ref[0, :, hd * HEAD_DIM:(hd + 1) * HEAD_DIM] = o[g * tq:(g + 1) * tq].astype(o_ref.dtype)

    accs = [_softmax_pv_bounded(queries(hk), k_ref, vt_ref, hk, shift_ref[0, 0]) for hk in range(N_KV_HEADS)]
    sums = [acc[HEAD_DIM:HEAD_DIM + 1] for acc in accs]
    for hk in range(N_KV_HEADS):
        store(hk, (accs[hk][:HEAD_DIM] / sums[hk]).T)

    @pl.when(jnp.logical_not(jnp.min(jnp.minimum(*sums)) >= MIN_SOFTMAX_SUM))
    def _():
        for hk in range(N_KV_HEADS):
            acc = _softmax_pv_online(queries(hk), k_ref, vt_ref, hk)
            store(hk, acc[:, :HEAD_DIM] / acc[:, HEAD_DIM:HEAD_DIM + 1])


def _post_attn_kernel(h_ref, o_ref, wo_ref, n2_ref, wg_ref, wu_ref, wd_ref, pn_ref, pgate_ref, p_ref, pproj_ref,
                      hout_ref):
    blocks = _row_blocks(h_ref)
    hs = [h_ref[rows, :] + _dot(o_ref[rows, :], wo_ref[...]) for rows in blocks]
    hs = _ffn_stage(hs, n2_ref, wg_ref, wu_ref, wd_ref)
    hs = _ple_stage(hs, pn_ref, pgate_ref, [p_ref[rows, :] for rows in blocks], pproj_ref)
    for rows, h in zip(blocks, hs):
        hout_ref[rows, :] = h


def _ffn_gmlp_kernel(h_ref, n1_ref, wg_ref, wu_ref, wd_ref, mixnorm_ref, win_ref, vnorm_ref, ws_ref, bs_ref,
                     wout_ref, hout_ref):
    blocks = _row_blocks(h_ref)
    hs = _ffn_stage([h_ref[rows, :] for rows in blocks], n1_ref, wg_ref, wu_ref, wd_ref)
    hs = _gmlp_stage(hs, mixnorm_ref, win_ref, vnorm_ref, ws_ref, bs_ref, wout_ref)
    for rows, h in zip(blocks, hs):
        hout_ref[rows, :] = h


def _ffn_ple_kernel(h_ref, n2_ref, wg_ref, wu_ref, wd_ref, pn_ref, pgate_ref, p_ref, pproj_ref, hout_ref):
    blocks = _row_blocks(h_ref)
    hs = _ffn_stage([h_ref[rows, :] for rows in blocks], n2_ref, wg_ref, wu_ref, wd_ref)
    hs = _ple_stage(hs, pn_ref, pgate_ref, [p_ref[rows, :] for rows in blocks], pproj_ref)
    for rows, h in zip(blocks, hs):
        hout_ref[rows, :] = h


def _resident(arr):
    nd = arr.ndim
    return pl.BlockSpec(arr.shape, lambda *_: (0,) * nd, pipeline_mode=pl.Buffered(1))


def _rows(tm, width):
    return pl.BlockSpec((tm, width), lambda i: (i, 0))


def _layer_rows(layer, tm, width):
    return pl.BlockSpec((None, tm, width), lambda i: (layer, i, 0))


def _params(n_axes=1):
    return pltpu.CompilerParams(dimension_semantics=("arbitrary",) * n_axes,
                                vmem_limit_bytes=VMEM_LIMIT_BYTES)


def _token_tile(seq_len, tile=TOKEN_TILE):
    tm = min(tile, seq_len)
    assert seq_len % tm == 0 and tm % ROW_BLOCK == 0 and ROW_BLOCK % CHUNK == 0
    return tm


def _pre_attn(h, seq_len, ffn, mixnorm, wqkv, qnorm, knorm, cos, sin):
    t = h.shape[0]
    tm = _token_tile(seq_len)
    tiles_per_seq = seq_len // tm
    n_tiles = t // tm
    weights = (*ffn, mixnorm, wqkv, qnorm, knorm)

    def cur(i):
        return jnp.minimum(i, n_tiles - 1)

    def prev(i):
        return jnp.maximum(i - 1, 0)

    table = pl.BlockSpec((tm, HEAD_DIM), lambda i: (prev(i) % tiles_per_seq, 0))
    vt_spec = pl.BlockSpec((1, 1, N_KV_HEADS * VT_ROWS, tm),
                           lambda i: (prev(i) // tiles_per_seq, prev(i) % tiles_per_seq, 0, 0))
    return pl.pallas_call(
        _pre_attn_kernel,
        grid=(n_tiles + 1,),
        in_specs=[pl.BlockSpec((tm, D_MODEL), lambda i: (cur(i), 0)), *[_resident(w) for w in weights], table, table],
        out_specs=[pl.BlockSpec((tm, D_MODEL), lambda i: (cur(i), 0)),
                   pl.BlockSpec((tm, Q_DIM), lambda i: (prev(i), 0)),
                   pl.BlockSpec((tm, KV_DIM), lambda i: (prev(i), 0)), vt_spec],
        out_shape=[jax.ShapeDtypeStruct((t, D_MODEL), F32), jax.ShapeDtypeStruct((t, Q_DIM), BF16),
                   jax.ShapeDtypeStruct((t, KV_DIM), BF16),
                   jax.ShapeDtypeStruct((t // seq_len, tiles_per_seq, N_KV_HEADS * VT_ROWS, tm), BF16)],
        scratch_shapes=[pltpu.VMEM((tm, QKV_DIM), F32)],
        compiler_params=_params(),
        name="pre_attn",
    )(h, *weights, cos, sin)


def _attention(q, k, vt, shift, batch, seq_len):
    tq = min(ATTN_Q_TILE, seq_len)
    assert seq_len % tq == 0
    q3 = q.reshape(batch, seq_len, Q_DIM)
    k3 = k.reshape(batch, seq_len, KV_DIM)
    qspec = pl.BlockSpec((1, tq, Q_DIM), lambda b, i: (b, i, 0))
    kspec = pl.BlockSpec((1, seq_len, KV_DIM), lambda b, i: (b, 0, 0))
    vspec = pl.BlockSpec((1, *vt.shape[1:]), lambda b, i: (b, 0, 0, 0))
    o = pl.pallas_call(
        _attn_kernel,
        grid=(batch, seq_len // tq),
        in_specs=[pl.BlockSpec(memory_space=pltpu.SMEM), qspec, kspec, vspec],
        out_specs=qspec,
        out_shape=jax.ShapeDtypeStruct((batch, seq_len, Q_DIM), BF16),
        compiler_params=_params(2),
        name="attention",
    )(shift, q3, k3, vt)
    return o.reshape(batch * seq_len, Q_DIM)


def _post_attn(h, o, wo, ffn, ple, p, layer, seq_len):
    t = h.shape[0]
    tm = _token_tile(seq_len)
    pn, pgate, pproj = ple
    return pl.pallas_call(
        _post_attn_kernel,
        grid=(t // tm,),
        in_specs=[_rows(tm, D_MODEL), _rows(tm, Q_DIM), _resident(wo), *[_resident(w) for w in ffn],
                  _resident(pn), _resident(pgate), _layer_rows(layer, tm, PLE_DIM), _resident(pproj)],
        out_specs=_rows(tm, D_MODEL),
        out_shape=jax.ShapeDtypeStruct((t, D_MODEL), F32),
        compiler_params=_params(),
        name="post_attn",
    )(h, o, wo, *ffn, pn, pgate, p, pproj)


def _ffn_gmlp(h, ffn, gmlp, seq_len):
    t = h.shape[0]
    tm = _token_tile(seq_len, LAYER1_TOKEN_TILE)
    weights = (*ffn, *gmlp)
    return pl.pallas_call(
        _ffn_gmlp_kernel,
        grid=(t // tm,),
        in_specs=[_rows(tm, D_MODEL), *[_resident(w) for w in weights]],
        out_specs=_rows(tm, D_MODEL),
        out_shape=jax.ShapeDtypeStruct((t, D_MODEL), F32),
        compiler_params=_params(),
        name="ffn_gmlp",
    )(h, *weights)


def _ffn_ple(h, ffn, ple, p, layer, seq_len):
    t = h.shape[0]
    tm = _token_tile(seq_len, LAYER1_TOKEN_TILE)
    pn, pgate, pproj = ple
    return pl.pallas_call(
        _ffn_ple_kernel,
        grid=(t // tm,),
        in_specs=[_rows(tm, D_MODEL), *[_resident(w) for w in ffn], _resident(pn), _resident(pgate),
                  _layer_rows(layer, tm, PLE_DIM), _resident(pproj)],
        out_specs=_rows(tm, D_MODEL),
        out_shape=jax.ShapeDtypeStruct((t, D_MODEL), F32),
        compiler_params=_params(),
        name="ffn_ple",
    )(h, *ffn, pn, pgate, p, pproj)


def _rope_tables(seq_len):
    t = jnp.arange(seq_len, dtype=jnp.int32)
    row = (t // GRID_W).astype(F32)
    col = (t % GRID_W).astype(F32)
    inv = ROPE_THETA ** (-jnp.arange(0, ROPE_HALF, 2, dtype=F32) / ROPE_HALF)
    ang_r = row[:, None] * inv
    ang_c = col[:, None] * inv
    cos = jnp.concatenate([jnp.cos(ang_r)] * 2 + [jnp.cos(ang_c)] * 2, axis=-1)
    sin = jnp.concatenate([-jnp.sin(ang_r), jnp.sin(ang_r), -jnp.sin(ang_c), jnp.sin(ang_c)], axis=-1)
    return cos, sin


def _score_bound(q_gain, k_gain):
    bound = jnp.max(jnp.abs(q_gain)) * jnp.max(jnp.abs(k_gain)) * (HEAD_DIM ** 0.5 * LOG2_E * (1.0 + 2.0 ** -7))
    return bound.astype(F32).reshape(1, 1)


def _row(vec):
    return vec.reshape(1, -1).astype(F32)


def _trunk(x, p, params):
    batch, seq_len, _ = x.shape
    t = batch * seq_len
    h = x.reshape(t, D_MODEL)
    p = p.reshape(p.shape[0], t, PLE_DIM)
    cos, sin = _rope_tables(seq_len)
    l0, l1 = params

    h, q, k, vt = _pre_attn(h, seq_len, l0["ffn1"], l0["mix_norm"], l0["w_qkv"], l0["q_norm"], l0["k_norm"], cos, sin)
    o = _attention(q, k, vt, l0["score_bound"], batch, seq_len)
    h = _post_attn(h, o, l0["w_o"], l0["ffn2"], l0["ple"], p, 0, seq_len)

    h = _ffn_gmlp(h, l1["ffn1"], l1["gmlp"], seq_len)
    h = _ffn_ple(h, l1["ffn2"], l1["ple"], p, 1, seq_len)
    return h.reshape(batch, seq_len, D_MODEL)


def kernel(x_prompt, x_sample, p_prompt, p_sample, ffn1_norm, ffn1_w_gate, ffn1_w_up, ffn1_w_down, mix_norm,
           attn_w_qkv, attn_q_norm, attn_k_norm, attn_w_o, gmlp_w_in, gmlp_v_norm, gmlp_w_s, gmlp_b_s, gmlp_w_out,
           ffn2_norm, ffn2_w_gate, ffn2_w_up, ffn2_w_down, ple_norm, ple_w_gate, ple_w_proj):
    def ffn(norm, wg, wu, wd, i):
        return (_row(norm[i]), wg[i].astype(BF16), wu[i].astype(BF16), wd[i].astype(BF16))

    def ple(i):
        return (_row(ple_norm[i]), ple_w_gate[i].astype(BF16), ple_w_proj[i].astype(BF16))

    bias = jnp.repeat(gmlp_b_s[0].T.astype(F32), GMLP_GROUP_W, axis=1)
    layer0 = dict(
        ffn1=ffn(ffn1_norm, ffn1_w_gate, ffn1_w_up, ffn1_w_down, 0),
        mix_norm=_row(mix_norm[0]), w_qkv=attn_w_qkv[0].astype(BF16),
        q_norm=_row(attn_q_norm[0]), k_norm=_row(attn_k_norm[0]), w_o=attn_w_o[0].astype(BF16),
        score_bound=_score_bound(attn_q_norm[0], attn_k_norm[0]),
        ffn2=ffn(ffn2_norm, ffn2_w_gate, ffn2_w_up, ffn2_w_down, 0), ple=ple(0))
    layer1 = dict(
        ffn1=ffn(ffn1_norm, ffn1_w_gate, ffn1_w_up, ffn1_w_down, 1),
        gmlp=(_row(mix_norm[1]), gmlp_w_in[0].astype(BF16), _row(gmlp_v_norm[0]), gmlp_w_s[0].astype(BF16), bias,
              gmlp_w_out[0].astype(BF16)),
        ffn2=ffn(ffn2_norm, ffn2_w_gate, ffn2_w_up, ffn2_w_down, 1), ple=ple(1))
    params = (layer0, layer1)
    return (_trunk(x_prompt, p_prompt, params), _trunk(x_sample, p_sample, params))
```

```python
import jax
import jax.numpy as jnp
from jax import lax
from jax.experimental import pallas as pl
from jax.experimental.pallas import tpu as pltpu

D_MODEL = 1024
N_Q_HEADS = 8
N_KV_HEADS = 2
HEAD_DIM = 128
GQA_GROUP = N_Q_HEADS // N_KV_HEADS
Q_DIM = N_Q_HEADS * HEAD_DIM
KV_DIM = N_KV_HEADS * HEAD_DIM
QKV_DIM = Q_DIM + 2 * KV_DIM
ROPE_HALF = HEAD_DIM // 2
ROPE_QUARTER = ROPE_HALF // 2
ROPE_THETA = 10000.0
GRID_W = 64
GMLP_WIDTH = D_MODEL
GMLP_GROUPS = 8
GMLP_GROUP_W = GMLP_WIDTH // GMLP_GROUPS
CHUNK = 128
D_FF = 2816
PLE_DIM = 256
EPS = 1e-6

BF16 = jnp.bfloat16
F32 = jnp.float32

VMEM_LIMIT_BYTES = 56 * 1024 * 1024
TOKEN_TILE = 512
ATTN_Q_TILE = 512
VT_ROWS = HEAD_DIM + 16
ROW_BLOCK = 256
LAYER1_TOKEN_TILE = 1024
LOG2_E = 1.4426950408889634
MIN_SOFTMAX_SUM = 2.0 ** -60


def _dot(a, b):
    return jnp.dot(a, b, preferred_element_type=F32)


def _rms(x, g):
    return x * lax.rsqrt(jnp.mean(x * x, axis=-1, keepdims=True) + EPS) * g


def _gelu_exact(x):
    return 0.5 * x * (1.0 + lax.erf(x * (2.0 ** -0.5)))


def _ffn_stage(hs, norm_ref, wg_ref, wu_ref, wd_ref):
    xns = [_rms(h, norm_ref[...]).astype(BF16) for h in hs]
    gs = [_dot(xn, wg_ref[...]) for xn in xns]
    us = [_dot(xn, wu_ref[...]) for xn in xns]
    acts = [(g * jax.nn.sigmoid(g) * u).astype(BF16) for g, u in zip(gs, us)]
    return [h + 0.5 * _dot(a, wd_ref[...]) for h, a in zip(hs, acts)]


def _ple_stage(hs, norm_ref, wgate_ref, ps, wproj_ref):
    xns = [_rms(h, norm_ref[...]).astype(BF16) for h in hs]
    gates = [jax.nn.sigmoid(_dot(xn, wgate_ref[...])) for xn in xns]
    return [h + gate * _dot(p.astype(BF16), wproj_ref[...]) for h, gate, p in zip(hs, gates, ps)]


def _rope_partner(x):
    lane = lax.broadcasted_iota(jnp.int32, x.shape, 1)
    first = (lane % ROPE_HALF) < ROPE_QUARTER
    up = pltpu.roll(x, HEAD_DIM - ROPE_QUARTER, 1)
    down = pltpu.roll(x, ROPE_QUARTER, 1)
    return jnp.where(first, up, down)


def _qk_head(x, gain, cos, sin, scale):
    y = _rms(x, gain)
    y = y * cos + _rope_partner(y) * sin
    if scale is not None:
        y = y * scale
    return y.astype(BF16)


def _qkv_heads(qkv, rows, qnorm_ref, knorm_ref, cos_ref, sin_ref, q_ref, k_ref, vt_ref):
    cos = cos_ref[rows, :]
    sin = sin_ref[rows, :]
    for hd in range(N_Q_HEADS):
        sl = slice(hd * HEAD_DIM, (hd + 1) * HEAD_DIM)
        q_ref[rows, sl] = _qk_head(qkv[:, sl], qnorm_ref[...], cos, sin, HEAD_DIM ** -0.5 * LOG2_E)
    for hd in range(N_KV_HEADS):
        sl = slice(hd * HEAD_DIM, (hd + 1) * HEAD_DIM)
        src = slice(Q_DIM + hd * HEAD_DIM, Q_DIM + (hd + 1) * HEAD_DIM)
        k_ref[rows, sl] = _qk_head(qkv[:, src], knorm_ref[...], cos, sin, None)
    ones = jnp.ones((VT_ROWS - HEAD_DIM, qkv.shape[0]), BF16)
    for hd in range(N_KV_HEADS):
        src = slice(Q_DIM + KV_DIM + hd * HEAD_DIM, Q_DIM + KV_DIM + (hd + 1) * HEAD_DIM)
        vt_ref[0, 0, hd * VT_ROWS:hd * VT_ROWS + HEAD_DIM, rows] = qkv[:, src].T.astype(BF16)
        vt_ref[0, 0, hd * VT_ROWS + HEAD_DIM:(hd + 1) * VT_ROWS, rows] = ones


def _spatial_gate(v, ws_ref, bs_ref):
    n_chunks = v.shape[0] // CHUNK
    cols = []
    for g in range(GMLP_GROUPS):
        gs = slice(g * GMLP_GROUP_W, (g + 1) * GMLP_GROUP_W)
        rhs = jnp.concatenate([v[c * CHUNK:(c + 1) * CHUNK, gs] for c in range(n_chunks)], axis=1)
        cols.append(_dot(ws_ref[g], rhs))
    rows = []
    for c in range(n_chunks):
        cs = slice(c * GMLP_GROUP_W, (c + 1) * GMLP_GROUP_W)
        rows.append(jnp.concatenate([cols[g][:, cs] for g in range(GMLP_GROUPS)], axis=1) + bs_ref[...])
    return jnp.concatenate(rows, axis=0)


def _gmlp_stage(hs, mixnorm_ref, win_ref, vnorm_ref, ws_ref, bs_ref, wout_ref):
    xns = [_rms(h, mixnorm_ref[...]).astype(BF16) for h in hs]
    zs = [_gelu_exact(_dot(xn, win_ref[...])) for xn in xns]
    vs = [_rms(z[:, GMLP_WIDTH:], vnorm_ref[...]).astype(BF16) for z in zs]
    svs = [_spatial_gate(v, ws_ref, bs_ref) for v in vs]
    gated = [(z[:, :GMLP_WIDTH] * sv).astype(BF16) for z, sv in zip(zs, svs)]
    return [h + _dot(x, wout_ref[...]) for h, x in zip(hs, gated)]


def _row_blocks(ref):
    return [slice(i, i + ROW_BLOCK) for i in range(0, ref.shape[0], ROW_BLOCK)]


def _pre_attn_kernel(h_ref, n1_ref, wg_ref, wu_ref, wd_ref, mixnorm_ref, wqkv_ref, qnorm_ref, knorm_ref,
                     cos_ref, sin_ref, hout_ref, q_ref, k_ref, vt_ref, qkv_ref):
    @pl.when(pl.program_id(0) == 0)
    def _():
        qkv_ref[...] = jnp.zeros(qkv_ref.shape, qkv_ref.dtype)

    blocks = _row_blocks(h_ref)
    hs = _ffn_stage([h_ref[rows, :] for rows in blocks], n1_ref, wg_ref, wu_ref, wd_ref)
    for rows in blocks:
        _qkv_heads(qkv_ref[rows, :], rows, qnorm_ref, knorm_ref, cos_ref, sin_ref, q_ref, k_ref, vt_ref)
    xns = [_rms(h, mixnorm_ref[...]).astype(BF16) for h in hs]
    for rows, h, xn in zip(blocks, hs, xns):
        hout_ref[rows, :] = h
        qkv_ref[rows, :] = _dot(xn, wqkv_ref[...])


def _nt_dot(a, b):
    return lax.dot_general(a, b, (((1,), (1,)), ((), ())), preferred_element_type=F32)


def _kv_chunk(k_ref, vt_ref, hk, j):
    tk = vt_ref.shape[3]
    return (k_ref[0, j * tk:(j + 1) * tk, hk * HEAD_DIM:(hk + 1) * HEAD_DIM],
            vt_ref[0, j, hk * VT_ROWS:(hk + 1) * VT_ROWS, :])


def _softmax_pv_bounded(q, k_ref, vt_ref, hk, shift):
    acc = None
    for j in range(vt_ref.shape[1]):
        k, vt = _kv_chunk(k_ref, vt_ref, hk, j)
        pv = _dot(vt, jnp.exp2(_nt_dot(k, q) - shift).astype(BF16))
        acc = pv if acc is None else acc + pv
    return acc


def _softmax_pv_online(q, k_ref, vt_ref, hk):
    m = acc = None
    for j in range(vt_ref.shape[1]):
        k, vt = _kv_chunk(k_ref, vt_ref, hk, j)
        s = _nt_dot(q, k)
        m_chunk = jnp.max(s, axis=-1, keepdims=True)
        m_new = m_chunk if m is None else jnp.maximum(m, m_chunk)
        pv = _nt_dot(jnp.exp2(s - m_new).astype(BF16), vt)
        acc = pv if m is None else jnp.exp2(m - m_new) * acc + pv
        m = m_new
    return acc


def _attn_kernel(shift_ref, q_ref, k_ref, vt_ref, o_ref):
    tq = q_ref.shape[1]

    def queries(hk):
        return jnp.concatenate([q_ref[0, :, (hk * GQA_GROUP + g) * HEAD_DIM:(hk * GQA_GROUP + g + 1) * HEAD_DIM]
                                for g in range(GQA_GROUP)], axis=0)

    def store(hk, o):
        for g in range(GQA_GROUP):
            hd = hk * GQA_GROUP + g
            o_ref[0, :, hd * HEAD_DIM:(hd + 1) * HEAD_DIM] = o[g * tq:(g + 1) * tq].astype(o_ref.dtype)

    accs = [_softmax_pv_bounded(queries(hk), k_ref, vt_ref, hk, shift_ref[0, 0]) for hk in range(N_KV_HEADS)]
    sums = [acc[HEAD_DIM:HEAD_DIM + 1] for acc in accs]
    for hk in range(N_KV_HEADS):
        store(hk, (accs[hk][:HEAD_DIM] / sums[hk]).T)

    @pl.when(jnp.logical_not(jnp.min(jnp.minimum(*sums)) >= MIN_SOFTMAX_SUM))
    def _():
        for hk in range(N_KV_HEADS):
            acc = _softmax_pv_online(queries(hk), k_ref, vt_ref, hk)
            store(hk, acc[:, :HEAD_DIM] / acc[:, HEAD_DIM:HEAD_DIM + 1])


def _post_attn_kernel(h_ref, o_ref, wo_ref, n2_ref, wg_ref, wu_ref, wd_ref, pn_ref, pgate_ref, p_ref, pproj_ref,
                      hout_ref):
    blocks = _row_blocks(h_ref)
    hs = [h_ref[rows, :] + _dot(o_ref[rows, :], wo_ref[...]) for rows in blocks]
    hs = _ffn_stage(hs, n2_ref, wg_ref, wu_ref, wd_ref)
    hs = _ple_stage(hs, pn_ref, pgate_ref, [p_ref[rows, :] for rows in blocks], pproj_ref)
    for rows, h in zip(blocks, hs):
        hout_ref[rows, :] = h


def _ffn_gmlp_kernel(h_ref, n1_ref, wg_ref, wu_ref, wd_ref, mixnorm_ref, win_ref, vnorm_ref, ws_ref, bs_ref,
                     wout_ref, hout_ref):
    blocks = _row_blocks(h_ref)
    hs = _ffn_stage([h_ref[rows, :] for rows in blocks], n1_ref, wg_ref, wu_ref, wd_ref)
    hs = _gmlp_stage(hs, mixnorm_ref, win_ref, vnorm_ref, ws_ref, bs_ref, wout_ref)
    for rows, h in zip(blocks, hs):
        hout_ref[rows, :] = h


def _ffn_ple_kernel(h_ref, n2_ref, wg_ref, wu_ref, wd_ref, pn_ref, pgate_ref, p_ref, pproj_ref, hout_ref):
    blocks = _row_blocks(h_ref)
    hs = _ffn_stage([h_ref[rows, :] for rows in blocks], n2_ref, wg_ref, wu_ref, wd_ref)
    hs = _ple_stage(hs, pn_ref, pgate_ref, [p_ref[rows, :] for rows in blocks], pproj_ref)
    for rows, h in zip(blocks, hs):
        hout_ref[rows, :] = h


def _resident(w):
    if isinstance(w, tuple):
        stack, layer = w
        tail = (0,) * (stack.ndim - 1)
        return pl.BlockSpec((None, *stack.shape[1:]), lambda *_: (layer, *tail), pipeline_mode=pl.Buffered(1))
    nd = w.ndim
    return pl.BlockSpec(w.shape, lambda *_: (0,) * nd, pipeline_mode=pl.Buffered(1))


def _arrays(ws):
    return [w[0] if isinstance(w, tuple) else w for w in ws]


def _rows(tm, width):
    return pl.BlockSpec((tm, width), lambda i: (i, 0))


def _layer_rows(layer, tm, width):
    return pl.BlockSpec((None, tm, width), lambda i: (layer, i, 0))


def _params(n_axes=1):
    return pltpu.CompilerParams(dimension_semantics=("arbitrary",) * n_axes,
                                vmem_limit_bytes=VMEM_LIMIT_BYTES)


def _token_tile(seq_len, tile=TOKEN_TILE):
    tm = min(tile, seq_len)
    assert seq_len % tm == 0 and tm % ROW_BLOCK == 0 and ROW_BLOCK % CHUNK == 0
    return tm


def _pre_attn(h, seq_len, ffn, mixnorm, wqkv, qnorm, knorm, cos, sin):
    t = h.shape[0]
    tm = _token_tile(seq_len)
    tiles_per_seq = seq_len // tm
    n_tiles = t // tm
    weights = (*ffn, mixnorm, wqkv, qnorm, knorm)

    def cur(i):
        return jnp.minimum(i, n_tiles - 1)

    def prev(i):
        return jnp.maximum(i - 1, 0)

    table = pl.BlockSpec((tm, HEAD_DIM), lambda i: (prev(i) % tiles_per_seq, 0))
    vt_spec = pl.BlockSpec((1, 1, N_KV_HEADS * VT_ROWS, tm),
                           lambda i: (prev(i) // tiles_per_seq, prev(i) % tiles_per_seq, 0, 0))
    return pl.pallas_call(
        _pre_attn_kernel,
        grid=(n_tiles + 1,),
        in_specs=[pl.BlockSpec((tm, D_MODEL), lambda i: (cur(i), 0)), *[_resident(w) for w in weights], table, table],
        out_specs=[pl.BlockSpec((tm, D_MODEL), lambda i: (cur(i), 0)),
                   pl.BlockSpec((tm, Q_DIM), lambda i: (prev(i), 0)),
                   pl.BlockSpec((tm, KV_DIM), lambda i: (prev(i), 0)), vt_spec],
        out_shape=[jax.ShapeDtypeStruct((t, D_MODEL), F32), jax.ShapeDtypeStruct((t, Q_DIM), BF16),
                   jax.ShapeDtypeStruct((t, KV_DIM), BF16),
                   jax.ShapeDtypeStruct((t // seq_len, tiles_per_seq, N_KV_HEADS * VT_ROWS, tm), BF16)],
        scratch_shapes=[pltpu.VMEM((tm, QKV_DIM), F32)],
        compiler_params=_params(),
        name="pre_attn",
    )(h, *_arrays(weights), cos, sin)


def _attention(q, k, vt, shift, batch, seq_len):
    tq = min(ATTN_Q_TILE, seq_len)
    assert seq_len % tq == 0
    q3 = q.reshape(batch, seq_len, Q_DIM)
    k3 = k.reshape(batch, seq_len, KV_DIM)
    qspec = pl.BlockSpec((1, tq, Q_DIM), lambda b, i: (b, i, 0))
    kspec = pl.BlockSpec((1, seq_len, KV_DIM), lambda b, i: (b, 0, 0))
    vspec = pl.BlockSpec((1, *vt.shape[1:]), lambda b, i: (b, 0, 0, 0))
    o = pl.pallas_call(
        _attn_kernel,
        grid=(batch, seq_len // tq),
        in_specs=[pl.BlockSpec(memory_space=pltpu.SMEM), qspec, kspec, vspec],
        out_specs=qspec,
        out_shape=jax.ShapeDtypeStruct((batch, seq_len, Q_DIM), BF16),
        compiler_params=_params(2),
        name="attention",
    )(shift, q3, k3, vt)
    return o.reshape(batch * seq_len, Q_DIM)


def _post_attn(h, o, wo, ffn, ple, p, layer, seq_len):
    t = h.shape[0]
    tm = _token_tile(seq_len)
    pn, pgate, pproj = ple
    return pl.pallas_call(
        _post_attn_kernel,
        grid=(t // tm,),
        in_specs=[_rows(tm, D_MODEL), _rows(tm, Q_DIM), _resident(wo), *[_resident(w) for w in ffn],
                  _resident(pn), _resident(pgate), _layer_rows(layer, tm, PLE_DIM), _resident(pproj)],
        out_specs=_rows(tm, D_MODEL),
        out_shape=jax.ShapeDtypeStruct((t, D_MODEL), F32),
        compiler_params=_params(),
        name="post_attn",
    )(h, o, *_arrays((wo, *ffn, pn, pgate)), p, *_arrays((pproj,)))


def _ffn_gmlp(h, ffn, gmlp, seq_len):
    t = h.shape[0]
    tm = _token_tile(seq_len, LAYER1_TOKEN_TILE)
    weights = (*ffn, *gmlp)
    return pl.pallas_call(
        _ffn_gmlp_kernel,
        grid=(t // tm,),
        in_specs=[_rows(tm, D_MODEL), *[_resident(w) for w in weights]],
        out_specs=_rows(tm, D_MODEL),
        out_shape=jax.ShapeDtypeStruct((t, D_MODEL), F32),
        compiler_params=_params(),
        name="ffn_gmlp",
    )(h, *_arrays(weights))


def _ffn_ple(h, ffn, ple, p, layer, seq_len):
    t = h.shape[0]
    tm = _token_tile(seq_len, LAYER1_TOKEN_TILE)
    pn, pgate, pproj = ple
    return pl.pallas_call(
        _ffn_ple_kernel,
        grid=(t // tm,),
        in_specs=[_rows(tm, D_MODEL), *[_resident(w) for w in ffn], _resident(pn), _resident(pgate),
                  _layer_rows(layer, tm, PLE_DIM), _resident(pproj)],
        out_specs=_rows(tm, D_MODEL),
        out_shape=jax.ShapeDtypeStruct((t, D_MODEL), F32),
        compiler_params=_params(),
        name="ffn_ple",
    )(h, *_arrays((*ffn, pn, pgate)), p, *_arrays((pproj,)))


def _rope_tables(seq_len):
    t = jnp.arange(seq_len, dtype=jnp.int32)
    row = (t // GRID_W).astype(F32)
    col = (t % GRID_W).astype(F32)
    inv = ROPE_THETA ** (-jnp.arange(0, ROPE_HALF, 2, dtype=F32) / ROPE_HALF)
    ang_r = row[:, None] * inv
    ang_c = col[:, None] * inv
    cos = jnp.concatenate([jnp.cos(ang_r)] * 2 + [jnp.cos(ang_c)] * 2, axis=-1)
    sin = jnp.concatenate([-jnp.sin(ang_r), jnp.sin(ang_r), -jnp.sin(ang_c), jnp.sin(ang_c)], axis=-1)
    return cos, sin


def _score_bound(q_gain, k_gain):
    bound = jnp.max(jnp.abs(q_gain)) * jnp.max(jnp.abs(k_gain)) * (HEAD_DIM ** 0.5 * LOG2_E * (1.0 + 2.0 ** -7))
    return bound.astype(F32).reshape(1, 1)


def _row(vec):
    return vec.reshape(1, -1).astype(F32)


def _trunk(x, p, params):
    batch, seq_len, _ = x.shape
    t = batch * seq_len
    h = x.reshape(t, D_MODEL)
    p = p.reshape(p.shape[0], t, PLE_DIM)
    cos, sin = _rope_tables(seq_len)
    l0, l1 = params

    h, q, k, vt = _pre_attn(h, seq_len, l0["ffn1"], l0["mix_norm"], l0["w_qkv"], l0["q_norm"], l0["k_norm"], cos, sin)
    o = _attention(q, k, vt, l0["score_bound"], batch, seq_len)
    h = _post_attn(h, o, l0["w_o"], l0["ffn2"], l0["ple"], p, 0, seq_len)

    h = _ffn_gmlp(h, l1["ffn1"], l1["gmlp"], seq_len)
    h = _ffn_ple(h, l1["ffn2"], l1["ple"], p, 1, seq_len)
    return h.reshape(batch, seq_len, D_MODEL)


def kernel(x_prompt, x_sample, p_prompt, p_sample, ffn1_norm, ffn1_w_gate, ffn1_w_up, ffn1_w_down, mix_norm,
           attn_w_qkv, attn_q_norm, attn_k_norm, attn_w_o, gmlp_w_in, gmlp_v_norm, gmlp_w_s, gmlp_b_s, gmlp_w_out,
           ffn2_norm, ffn2_w_gate, ffn2_w_up, ffn2_w_down, ple_norm, ple_w_gate, ple_w_proj):
    ffn1_w = [w.astype(BF16) for w in (ffn1_w_gate, ffn1_w_up, ffn1_w_down)]
    ffn2_w = [w.astype(BF16) for w in (ffn2_w_gate, ffn2_w_up, ffn2_w_down)]
    ple_w = [w.astype(BF16) for w in (ple_w_gate, ple_w_proj)]

    def ffn(norm, ws, i):
        return (_row(norm[i]), *[(w, i) for w in ws])

    def ple(i):
        return (_row(ple_norm[i]), *[(w, i) for w in ple_w])

    bias = jnp.repeat(gmlp_b_s[0].T.astype(F32), GMLP_GROUP_W, axis=1)
    layer0 = dict(
        ffn1=ffn(ffn1_norm, ffn1_w, 0),
        mix_norm=_row(mix_norm[0]), w_qkv=(attn_w_qkv.astype(BF16), 0),
        q_norm=_row(attn_q_norm[0]), k_norm=_row(attn_k_norm[0]), w_o=(attn_w_o.astype(BF16), 0),
        score_bound=_score_bound(attn_q_norm[0], attn_k_norm[0]),
        ffn2=ffn(ffn2_norm, ffn2_w, 0), ple=ple(0))
    layer1 = dict(
        ffn1=ffn(ffn1_norm, ffn1_w, 1),
        gmlp=(_row(mix_norm[1]), (gmlp_w_in.astype(BF16), 0), _row(gmlp_v_norm[0]), (gmlp_w_s.astype(BF16), 0), bias,
              (gmlp_w_out.astype(BF16), 0)),
        ffn2=ffn(ffn2_norm, ffn2_w, 1), ple=ple(1))
    params = (layer0, layer1)
    return (_trunk(x_prompt, p_prompt, params), _trunk(x_sample, p_sample, params))
```
